```python
import jax, jax.numpy as jnp
from jax import lax
import numpy as np

D_MODEL = 1024
BATCH = 8
SEQ = 2048
DEPTH = 4
DEC_BATCH = 128
DEC_SEQ = 4
PAST_LEN = 2048
PAGE_SIZE = 128

HEAD_DIM = 64
H_MOBA = 6
H_NSA = 6
KV_NSA = 2
GROUP = H_NSA // KV_NSA
H_SB = 4
D_MOBA = H_MOBA * HEAD_DIM
D_NSA = H_NSA * HEAD_DIM
D_NSA_KV = KV_NSA * HEAD_DIM
D_SB = H_SB * HEAD_DIM
D_MIX = D_MOBA + D_NSA + D_SB
D_FF = 4 * D_MODEL
MOBA_BLOCK = 256
MOBA_TOPK = 3
CMP_LEN = 32
CMP_STRIDE = 16
CMP_HID = 128
SEL_BLOCK = 64
SEL_TOPK = 8
WINDOW = 512
N_GATES = 3 * H_NSA
Q_BLOCK = 128
MOBA_Q_BLOCK = 32
SPLIT_SIZES = (D_MOBA, D_MOBA, D_MOBA, D_NSA, D_NSA_KV, D_NSA_KV, D_NSA_KV, D_NSA_KV, D_NSA_KV, D_NSA_KV, N_GATES, D_SB, D_SB, D_SB)
IN_COLS = sum(SPLIT_SIZES)
EPS = 1e-6
NEG = -1e30
BIG = 1e30

kernel_name = "hymba_moba_nsa_stickbreaking_decode_step"

F32 = jnp.float32
SCALE = HEAD_DIM ** -0.5


def rms_norm(x, g):
    xf = x.astype(F32)
    y = xf * lax.rsqrt(jnp.mean(xf * xf, axis=-1, keepdims=True) + EPS)
    return (y * g.astype(F32)).astype(x.dtype)


def group_rms_norm(parts, g):
    normed = [p.astype(F32) * lax.rsqrt(jnp.mean(jnp.square(p.astype(F32)), axis=-1, keepdims=True) + EPS) for p in parts]
    return (jnp.concatenate(normed, axis=-1) * g.astype(F32)).astype(parts[0].dtype)


def masked_softmax(s, mask):
    p = jax.nn.softmax(jnp.where(mask, s, NEG), axis=-1)
    return jnp.where(mask, p, 0.0)


def alibi_slopes():
    n = H_MOBA + H_NSA
    s = 2.0 ** (-8.0 * np.arange(1, n + 1) / n)
    return jnp.asarray(s[0::2], F32), jnp.asarray(s[1::2], F32)


def split_points():
    return [int(v) for v in np.cumsum(SPLIT_SIZES)[:-1]]


def sweep_queries(f, q, q_pos, block):
    B, Q, H, D = q.shape
    if Q <= block or Q % block:
        return f(q, q_pos)
    n = Q // block
    qs = q.reshape(B, n, block, H, D).transpose(1, 0, 2, 3, 4)
    ps = q_pos.reshape(n, block)
    out = lax.map(lambda a: f(a[0], a[1]), (qs, ps))
    return out.transpose(1, 0, 2, 3, 4).reshape(B, Q, H, -1)


def to_blocks(rows, blk):
    B, T, H, D = rows.shape
    nb = -(-T // blk)
    rows = jnp.pad(rows, ((0, 0), (0, nb * blk - T), (0, 0), (0, 0)))
    return rows.reshape(B, nb, blk, H, D).transpose(0, 3, 1, 2, 4)


def gather_blocks(blocks, idx):
    return jax.vmap(jax.vmap(lambda kb, ix: kb[ix]))(blocks, idx)


def gather_pages(pool, page_table):
    g = pool[page_table]
    return g.reshape((g.shape[0], g.shape[1] * g.shape[2]) + g.shape[3:])


def moba_attention(q, k, v, q_pos, slopes):
    kb, vb = to_blocks(k, MOBA_BLOCK), to_blocks(v, MOBA_BLOCK)
    nb = kb.shape[2]
    k_mean = jnp.mean(kb.astype(F32), axis=3)
    n_sel = max(1, min(MOBA_TOPK, nb))
    blk_ids = jnp.arange(nb)
    offs = jnp.arange(MOBA_BLOCK)

    def chunk(q_c, pos_c):
        B, Qc = q_c.shape[:2]
        own = pos_c // MOBA_BLOCK
        gate = jnp.einsum('bqhd,bhnd->bhqn', q_c.astype(F32), k_mean)
        gate = jnp.where(blk_ids[None, :] < own[:, None], gate, NEG)
        _, top = lax.top_k(gate, n_sel)
        idx = jnp.concatenate([top, jnp.broadcast_to(own, top.shape[:-1])[..., None]], axis=-1)
        valid = jnp.concatenate([jnp.arange(n_sel)[None, :] < own[:, None], jnp.ones((Qc, 1), bool)], axis=-1)
        kg, vg = gather_blocks(kb, idx), gather_blocks(vb, idx)
        kpos = idx[..., None] * MOBA_BLOCK + offs
        dist = (pos_c[:, None, None] - kpos).astype(F32)
        mask = valid[:, :, None] & (dist >= 0)
        s = jnp.einsum('bqhd,bhqsld->bhqsl', q_c, kg, preferred_element_type=F32) * SCALE - slopes[:, None, None, None] * dist
        _, H, _, S, L = s.shape
        p = masked_softmax(s.reshape(B, H, Qc, S * L), mask.reshape(B, H, Qc, S * L))
        return jnp.einsum('bhqk,bhqkd->bqhd', p.astype(v.dtype), vg.reshape(B, H, Qc, S * L, HEAD_DIM))

    return sweep_queries(chunk, q, q_pos, MOBA_Q_BLOCK)


def nsa_compress(rows, pos_emb, w1, w2):
    B, T, G, D = rows.shape
    n_cmp = (T - CMP_LEN) // CMP_STRIDE + 1
    idx = np.arange(n_cmp)[:, None] * CMP_STRIDE + np.arange(CMP_LEN)[None, :]
    blk = rows[:, idx] + pos_emb[:, None, :]
    blk = blk.transpose(0, 1, 3, 2, 4).reshape(B, n_cmp, G, CMP_LEN * D)
    return jax.nn.relu(blk @ w1) @ w2


def nsa_cmp_sel_attention(q, kc, vc, ks, vs, q_pos, slopes, cmp_pos, w_ck1, w_ck2, w_cv1, w_cv2):
    k_cmp = nsa_compress(kc, cmp_pos[0], w_ck1, w_ck2)
    v_cmp = nsa_compress(vc, cmp_pos[1], w_cv1, w_cv2)
    n_cmp = k_cmp.shape[1]
    cmp_end = jnp.arange(n_cmp) * CMP_STRIDE + (CMP_LEN - 1)
    ksb, vsb = to_blocks(ks, SEL_BLOCK), to_blocks(vs, SEL_BLOCK)
    n_blk = ksb.shape[2]
    starts = np.arange(n_cmp) * CMP_STRIDE
    sbs = np.arange(n_blk) * SEL_BLOCK
    overlap = jnp.asarray((starts[:, None] < sbs[None, :] + SEL_BLOCK) & (starts[:, None] + CMP_LEN > sbs[None, :]), F32)
    n_sel = min(SEL_TOPK, n_blk)
    sl = slopes.reshape(KV_NSA, GROUP)
    blk_ids = jnp.arange(n_blk)

    def chunk(q_c, pos_c):
        B, Qc = q_c.shape[:2]
        qg = q_c.reshape(B, Qc, KV_NSA, GROUP, HEAD_DIM)
        dist_c = (pos_c[:, None] - cmp_end[None, :]).astype(F32)
        s = jnp.einsum('bqgrd,bngd->bgrqn', qg, k_cmp, preferred_element_type=F32) * SCALE - sl[:, :, None, None] * dist_c
        p_cmp = masked_softmax(s, dist_c >= 0)
        o_cmp = jnp.einsum('bgrqn,bngd->bqgrd', p_cmp.astype(v_cmp.dtype), v_cmp)
        imp = jnp.einsum('bgrqn,nm->bgqm', p_cmp, overlap)
        own = pos_c // SEL_BLOCK
        forced = (blk_ids[None, :] == own[:, None]) | (blk_ids[None, :] == 0)
        imp = jnp.where(forced, BIG, jnp.where(blk_ids[None, :] <= own[:, None], imp, -BIG))
        _, top = lax.top_k(imp, n_sel)
        valid = jnp.arange(n_sel)[None, :] <= own[:, None]
        kg, vg = gather_blocks(ksb, top), gather_blocks(vsb, top)
        kpos = top[..., None] * SEL_BLOCK + jnp.arange(SEL_BLOCK)
        dist = (pos_c[:, None, None] - kpos).astype(F32)
        s = jnp.einsum('bqgrd,bgqsld->bgrqsl', qg, kg, preferred_element_type=F32) * SCALE - sl[:, :, None, None, None] * dist[:, :, None]
        mask = jnp.broadcast_to((valid[:, :, None] & (dist >= 0))[:, :, None], s.shape)
        S = s.shape[-2] * s.shape[-1]
        p = masked_softmax(s.reshape(s.shape[:4] + (S,)), mask.reshape(s.shape[:4] + (S,)))
        o_sel = jnp.einsum('bgrqk,bgqkd->bqgrd', p.astype(vs.dtype), vg.reshape(B, KV_NSA, Qc, S, HEAD_DIM))
        return jnp.concatenate([o_cmp.reshape(B, Qc, H_NSA, HEAD_DIM), o_sel.reshape(B, Qc, H_NSA, HEAD_DIM)], axis=-1)

    return sweep_queries(chunk, q, q_pos, Q_BLOCK)


def window_attention(q, k, v, q_pos, k_pos, slopes):
    B, Q = q.shape[:2]
    qg = q.reshape(B, Q, KV_NSA, GROUP, HEAD_DIM)
    sl = slopes.reshape(KV_NSA, GROUP)
    dist = (q_pos[:, None] - k_pos[None, :]).astype(F32)
    mask = (dist >= 0) & (dist < WINDOW) & (k_pos[None, :] >= 0)
    s = jnp.einsum('bqgrd,bkgd->bgrqk', qg, k, preferred_element_type=F32) * SCALE - sl[:, :, None, None] * dist
    p = masked_softmax(s, mask)
    return jnp.einsum('bgrqk,bkgd->bqgrd', p.astype(v.dtype), v).reshape(B, Q, H_NSA, HEAD_DIM)


def window_prompt(q, k, v, q_pos, slopes):
    pad = ((0, 0), (WINDOW, 0), (0, 0), (0, 0))
    kp, vp = jnp.pad(k, pad), jnp.pad(v, pad)

    def chunk(q_c, pos_c):
        n = WINDOW + q_c.shape[1]
        start = pos_c[0]
        kc = lax.dynamic_slice_in_dim(kp, start, n, axis=1)
        vc = lax.dynamic_slice_in_dim(vp, start, n, axis=1)
        return window_attention(q_c, kc, vc, pos_c, start - WINDOW + jnp.arange(n), slopes)

    return sweep_queries(chunk, q, q_pos, Q_BLOCK)


def stick_breaking_attention(q, k, v, q_pos):
    k_pos = jnp.arange(k.shape[1])

    def chunk(q_c, pos_c):
        z = jnp.einsum('bqhd,bthd->bhqt', q_c, k, preferred_element_type=F32) * SCALE
        causal = k_pos[None, :] < pos_c[:, None]
        log_1mb = jnp.where(causal, jax.nn.log_sigmoid(-z), 0.0)
        after = lax.cumsum(log_1mb, axis=3, reverse=True) - log_1mb
        a = jnp.where(causal, jnp.exp(jax.nn.log_sigmoid(z) + after), 0.0)
        return jnp.einsum('bhqt,bthd->bqhd', a.astype(v.dtype), v)

    return sweep_queries(chunk, q, q_pos, Q_BLOCK)


def squared_relu_mlp(h, w_up, w_down):
    return jnp.square(jax.nn.relu(h @ w_up)) @ w_down


def trunk_layer(x, q_pos, past, g_attn, w_in, cmp_pos, w_ck1, w_ck2, w_cv1, w_cv2, g_mix, w_out, g_ffn, w_up, w_down, slopes_a, slopes_b):
    B, Q, _ = x.shape
    h = rms_norm(x, g_attn)
    (qa, ka, va, qb, kc, vc, ks, vs, kw, vw, gt, qc, kcs, vcs) = jnp.split(h @ w_in, split_points(), axis=-1)
    hd = lambda t: t.reshape(B, Q, -1, HEAD_DIM)
    new_moba = jnp.stack([hd(ka), hd(va)], axis=2)
    new_cmp = jnp.stack([hd(kc), hd(vc)], axis=2)
    new_sel = jnp.stack([hd(ks), hd(vs)], axis=2)
    new_win = jnp.stack([hd(kw), hd(vw)], axis=2)
    new_sb = jnp.stack([hd(kcs), hd(vcs)], axis=2)
    if past is None:
        all_moba, all_cmp, all_sel, all_sb = new_moba, new_cmp, new_sel, new_sb
    else:
        past_moba, past_cmp, past_sel, past_sb, win_buf = past
        cat = lambda a, b: jnp.concatenate([a, b], axis=1)
        all_moba, all_cmp, all_sel, all_sb = cat(past_moba, new_moba), cat(past_cmp, new_cmp), cat(past_sel, new_sel), cat(past_sb, new_sb)
    q_b = hd(qb)
    o_a = moba_attention(hd(qa), all_moba[:, :, 0], all_moba[:, :, 1], q_pos, slopes_a)
    o_cs = nsa_cmp_sel_attention(q_b, all_cmp[:, :, 0], all_cmp[:, :, 1], all_sel[:, :, 0], all_sel[:, :, 1], q_pos, slopes_b, cmp_pos, w_ck1, w_ck2, w_cv1, w_cv2)
    o_cmp, o_sel = jnp.split(o_cs, 2, axis=-1)
    if past is None:
        o_win = window_prompt(q_b, hd(kw), hd(vw), q_pos, slopes_b)
        win_state = new_win[:, Q - min(WINDOW, Q):]
    else:
        all_win = jnp.concatenate([win_buf, new_win], axis=1)
        k_pos = q_pos[0] - win_buf.shape[1] + jnp.arange(all_win.shape[1])
        o_win = window_attention(q_b, all_win[:, :, 0], all_win[:, :, 1], q_pos, k_pos, slopes_b)
        n_w = all_win.shape[1]
        win_state = all_win[:, n_w - min(WINDOW, n_w):]
    g = jax.nn.sigmoid(gt.astype(F32)).reshape(B, Q, 3, H_NSA, 1)
    o_b = (g[:, :, 0] * o_cmp + g[:, :, 1] * o_sel + g[:, :, 2] * o_win).astype(o_a.dtype)
    o_c = stick_breaking_attention(hd(qc), all_sb[:, :, 0], all_sb[:, :, 1], q_pos)
    mixed = group_rms_norm([o_a.reshape(B, Q, D_MOBA), o_b.reshape(B, Q, D_NSA), o_c.reshape(B, Q, D_SB)], g_mix)
    x = x + mixed @ w_out
    x = x + squared_relu_mlp(rms_norm(x, g_ffn), w_up, w_down)
    return x, (new_moba, new_cmp, new_sel, new_sb, win_state)


def setup_inputs(seed: int = 0) -> dict:
    key = jax.random.key(seed)
    ks = jax.random.split(key, 24)
    n_pages = PAST_LEN // PAGE_SIZE
    n_phys = (DEC_BATCH * n_pages * 5) // 4
    wb = min(WINDOW, PAST_LEN)
    nrm = lambda k, shape, scale=1.0: scale * jax.random.normal(k, shape, F32)
    gain = lambda k, shape: 1.0 + 0.02 * jax.random.normal(k, shape, F32)
    page_table = jax.random.permutation(ks[7], n_phys)[: DEC_BATCH * n_pages].reshape(DEC_BATCH, n_pages).astype(jnp.int32)
    return {
        "x_prompt": nrm(ks[0], (BATCH, SEQ, D_MODEL)),
        "x_sample": nrm(ks[1], (DEC_BATCH, DEC_SEQ, D_MODEL)),
        "cache_moba_kv": nrm(ks[2], (DEPTH, n_phys, PAGE_SIZE, 2, H_MOBA, HEAD_DIM)),
        "cache_nsa_cmp_kv": nrm(ks[3], (DEPTH, n_phys, PAGE_SIZE, 2, KV_NSA, HEAD_DIM)),
        "cache_nsa_sel_kv": nrm(ks[4], (DEPTH, n_phys, PAGE_SIZE, 2, KV_NSA, HEAD_DIM)),
        "cache_sb_kv": nrm(ks[5], (DEPTH, n_phys, PAGE_SIZE, 2, H_SB, HEAD_DIM)),
        "state_nsa_win_kv": nrm(ks[6], (DEPTH, DEC_BATCH, wb, 2, KV_NSA, HEAD_DIM)),
        "page_table": page_table,
        "norm_attn": gain(ks[8], (DEPTH, D_MODEL)),
        "w_in": nrm(ks[9], (DEPTH, D_MODEL, IN_COLS), D_MODEL ** -0.5),
        "cmp_pos": nrm(ks[10], (DEPTH, 2, CMP_LEN, HEAD_DIM), 0.1),
        "w_cmp_k1": nrm(ks[11], (DEPTH, CMP_LEN * HEAD_DIM, CMP_HID), (CMP_LEN * HEAD_DIM) ** -0.5),
        "w_cmp_k2": nrm(ks[12], (DEPTH, CMP_HID, HEAD_DIM), (CMP_HID / 2) ** -0.5),
        "w_cmp_v1": nrm(ks[13], (DEPTH, CMP_LEN * HEAD_DIM, CMP_HID), (CMP_LEN * HEAD_DIM) ** -0.5),
        "w_cmp_v2": nrm(ks[14], (DEPTH, CMP_HID, HEAD_DIM), (CMP_HID / 2) ** -0.5),
        "norm_mix": gain(ks[15], (DEPTH, D_MIX)),
        "w_out": nrm(ks[16], (DEPTH, D_MIX, D_MODEL), D_MIX ** -0.5),
        "norm_ffn": gain(ks[17], (DEPTH, D_MODEL)),
        "w_up": nrm(ks[18], (DEPTH, D_MODEL, D_FF), D_MODEL ** -0.5),
        "w_down": nrm(ks[19], (DEPTH, D_FF, D_MODEL), D_FF ** -0.5),
        "norm_final": gain(ks[20], (D_MODEL,)),
    }


def reference(x_prompt, x_sample, cache_moba_kv, cache_nsa_cmp_kv, cache_nsa_sel_kv, cache_sb_kv, state_nsa_win_kv, page_table,
              norm_attn, w_in, cmp_pos, w_cmp_k1, w_cmp_k2, w_cmp_v1, w_cmp_v2, norm_mix, w_out, norm_ffn, w_up, w_down, norm_final):
    slopes_a, slopes_b = alibi_slopes()
    past_len = page_table.shape[1] * PAGE_SIZE
    pos_p = jnp.arange(x_prompt.shape[1], dtype=jnp.int32)
    pos_s = past_len + jnp.arange(x_sample.shape[1], dtype=jnp.int32)
    xp, xs = x_prompt, x_sample
    st_p = [[] for _ in range(5)]
    st_s = [[] for _ in range(5)]
    for l in range(DEPTH):
        w = (norm_attn[l], w_in[l], cmp_pos[l], w_cmp_k1[l], w_cmp_k2[l], w_cmp_v1[l], w_cmp_v2[l], norm_mix[l], w_out[l], norm_ffn[l], w_up[l], w_down[l])
        past = (gather_pages(cache_moba_kv[l], page_table), gather_pages(cache_nsa_cmp_kv[l], page_table),
                gather_pages(cache_nsa_sel_kv[l], page_table), gather_pages(cache_sb_kv[l], page_table), state_nsa_win_kv[l])
        xp, new_p = trunk_layer(xp, pos_p, None, *w, slopes_a, slopes_b)
        xs, new_s = trunk_layer(xs, pos_s, past, *w, slopes_a, slopes_b)
        for i in range(5):
            st_p[i].append(new_p[i])
            st_s[i].append(new_s[i])
    y_prompt = rms_norm(xp, norm_final)
    y_sample = rms_norm(xs, norm_final)
    new_state = []
    for i in range(5):
        new_state.append(jnp.stack(st_p[i]))
        new_state.append(jnp.stack(st_s[i]))
    return (y_prompt, y_sample, *new_state)
```

```python
import functools

import numpy as np
import jax
import jax.numpy as jnp
from jax import lax
from jax.experimental import pallas as pl
from jax.experimental.pallas import tpu as pltpu

F32 = jnp.float32
BF16 = jnp.bfloat16

HEAD_DIM = 64
H_MOBA = 6
H_NSA = 6
KV_NSA = 2
GROUP = H_NSA // KV_NSA
H_SB = 4
D_MOBA = H_MOBA * HEAD_DIM
D_NSA = H_NSA * HEAD_DIM
D_NSA_KV = KV_NSA * HEAD_DIM
D_SB = H_SB * HEAD_DIM
MOBA_BLOCK = 256
MOBA_TOPK = 3
CMP_LEN = 32
CMP_STRIDE = 16
CMP_HID = 128
SEL_BLOCK = 64
SEL_TOPK = 8
WINDOW = 512
N_GATES = 3 * H_NSA
PAGE = 128
EPS = 1e-6
NEG = -1e30
BIG = 1e30
SCALE = HEAD_DIM ** -0.5

LANES = 128
ROW_TILE = 512
NEW_ROWS = 16
Q_ROWS = 8
CMP_GROUP = 4
VMEM_LIMIT = 56 * 1024 * 1024

_PROJ_OUT = (("q_moba", D_MOBA), ("kv_moba", 2 * D_MOBA), ("q_nsa", D_NSA), ("kv_cmp", 2 * D_NSA_KV),
             ("kv_sel", 2 * D_NSA_KV), ("kv_win", 2 * D_NSA_KV), ("q_sb", D_SB), ("kv_sb", 2 * D_SB),
             ("gates", LANES))
_NSA_HEAD_ORDER = (0, 3, 1, 4, 2, 5)


def _alibi_slopes():
    n = H_MOBA + H_NSA
    s = 2.0 ** (-8.0 * np.arange(1, n + 1) / n)
    return jnp.asarray(s[0::2], F32), jnp.asarray(s[1::2], F32)


def _nt_dot(a, b, precision=None):
    return lax.dot_general(a, b, (((1,), (1,)), ((), ())), preferred_element_type=F32, precision=precision)


def _softplus(z):
    return jnp.maximum(z, 0.0) + jnp.log(1.0 + jnp.exp(-jnp.abs(z)))


def _cparams(sem):
    return pltpu.CompilerParams(dimension_semantics=sem, vmem_limit_bytes=VMEM_LIMIT)


def _in_proj_kernel(x_ref, g_ref, w_ref, *out_refs):
    x = x_ref[...]
    h = x * lax.rsqrt(jnp.mean(x * x, axis=-1, keepdims=True) + EPS) * g_ref[...]
    hb = h.astype(BF16)
    off = 0
    for ref, (_, width) in zip(out_refs, _PROJ_OUT):
        ref[...] = jnp.dot(hb, w_ref[:, off:off + width], preferred_element_type=F32)
        off += width


def _in_proj(x2d, g, w):
    n, d = x2d.shape
    tm = min(ROW_TILE, n)
    ncol = w.shape[1]
    return pl.pallas_call(
        _in_proj_kernel,
        grid=(n // tm,),
        in_specs=[pl.BlockSpec((tm, d), lambda i: (i, 0)),
                  pl.BlockSpec((1, d), lambda i: (0, 0)),
                  pl.BlockSpec((d, ncol), lambda i: (0, 0))],
        out_specs=[pl.BlockSpec((tm, wd), lambda i: (i, 0)) for _, wd in _PROJ_OUT],
        out_shape=[jax.ShapeDtypeStruct((n, wd), F32) for _, wd in _PROJ_OUT],
        compiler_params=_cparams(("parallel",)),
        name="in_proj",
    )(x2d, g.reshape(1, d), w)


def _post_kernel(oa_ref, ob_ref, oc_ref, x_ref, gmix_ref, wout_ref, gffn_ref, wup_ref, wdown_ref, out_ref, *, ff_chunk):
    def gnorm(o):
        return o * lax.rsqrt(jnp.mean(o * o, axis=-1, keepdims=True) + EPS)

    mixed = jnp.concatenate([gnorm(oa_ref[...]), gnorm(ob_ref[...]), gnorm(oc_ref[...])], axis=-1) * gmix_ref[...]
    x1 = x_ref[...] + jnp.dot(mixed.astype(BF16), wout_ref[...], preferred_element_type=F32)
    h2 = (x1 * lax.rsqrt(jnp.mean(x1 * x1, axis=-1, keepdims=True) + EPS) * gffn_ref[...]).astype(BF16)
    acc = x1
    d_ff = wup_ref.shape[1]
    for c in range(d_ff // ff_chunk):
        hid = jnp.dot(h2, wup_ref[:, c * ff_chunk:(c + 1) * ff_chunk], preferred_element_type=F32)
        hid = jnp.square(jnp.maximum(hid, 0.0)).astype(BF16)
        acc = acc + jnp.dot(hid, wdown_ref[c * ff_chunk:(c + 1) * ff_chunk, :], preferred_element_type=F32)
    out_ref[...] = acc


def _post(oa, ob, oc, x2d, gmix, wout, gffn, wup, wdown):
    n, d = x2d.shape
    tm = min(ROW_TILE, n)
    dmix = wout.shape[0]
    dff = wup.shape[1]
    const = lambda shape: pl.BlockSpec(shape, lambda i: (0, 0), pipeline_mode=pl.Buffered(1))
    row = lambda wd: pl.BlockSpec((tm, wd), lambda i: (i, 0))
    return pl.pallas_call(
        functools.partial(_post_kernel, ff_chunk=1024),
        grid=(n // tm,),
        in_specs=[row(oa.shape[1]), row(ob.shape[1]), row(oc.shape[1]), row(d),
                  const((1, dmix)), const((dmix, d)), const((1, d)), const((d, dff)), const((dff, d))],
        out_specs=row(d),
        out_shape=jax.ShapeDtypeStruct((n, d), F32),
        compiler_params=_cparams(("parallel",)),
        name="post_mlp",
    )(oa, ob, oc, x2d, gmix.reshape(1, dmix), wout, gffn.reshape(1, d), wup, wdown)


def _final_norm_kernel(x_ref, g_ref, o_ref):
    x = x_ref[...]
    o_ref[...] = x * lax.rsqrt(jnp.mean(x * x, axis=-1, keepdims=True) + EPS) * g_ref[...]


def _final_norm(x2d, g):
    n, d = x2d.shape
    tm = min(ROW_TILE, n)
    return pl.pallas_call(
        _final_norm_kernel,
        grid=(n // tm,),
        in_specs=[pl.BlockSpec((tm, d), lambda i: (i, 0)), pl.BlockSpec((1, d), lambda i: (0, 0))],
        out_specs=pl.BlockSpec((tm, d), lambda i: (i, 0)),
        out_shape=jax.ShapeDtypeStruct((n, d), F32),
        compiler_params=_cparams(("parallel",)),
        name="final_norm",
    )(x2d, g.reshape(1, d))


def _compress_kernel(pt_ref, kpage_ref, vpage_ref, pos_ref, w1_ref, w2_ref, out_ref, x_scr, *, n_pages):
    del pt_ref
    s = pl.program_id(1)
    r = s // n_pages
    p = s % n_pages
    rows_per_page = PAGE // CMP_STRIDE
    base = pl.multiple_of(r * (n_pages * rows_per_page) + p * rows_per_page, rows_per_page)
    for l in range(CMP_STRIDE):
        x_scr[l, pl.ds(base, rows_per_page), 0:LANES] = kpage_ref[pl.ds(l, rows_per_page, stride=CMP_STRIDE), :]
        x_scr[l, pl.ds(base, rows_per_page), LANES:2 * LANES] = vpage_ref[pl.ds(l, rows_per_page, stride=CMP_STRIDE), :]

    @pl.when(s == pl.num_programs(1) - 1)
    def _():
        rows = x_scr.shape[1]
        acc_a = jnp.zeros((rows, w1_ref.shape[2]), F32)
        acc_b = jnp.zeros((rows, w1_ref.shape[2]), F32)
        for l in range(CMP_STRIDE):
            x = x_scr[l]
            acc_a = acc_a + jnp.dot((x + pos_ref[l]).astype(BF16), w1_ref[l], preferred_element_type=F32)
            acc_b = acc_b + jnp.dot((x + pos_ref[CMP_STRIDE + l]).astype(BF16), w1_ref[CMP_STRIDE + l],
                                    preferred_element_type=F32)
        hidden = acc_a + pltpu.roll(acc_b, rows - 1, 0)
        out_ref[...] = jnp.dot(jnp.maximum(hidden, 0.0).astype(BF16), w2_ref[...], preferred_element_type=F32)


def _compress(pool, page_ids, n_req, n_pages, pos, w1, w2):
    group = min(CMP_GROUP, n_req)
    rows_req = n_pages * (PAGE // CMP_STRIDE)
    width = pool.shape[2]
    grid_spec = pltpu.PrefetchScalarGridSpec(
        num_scalar_prefetch=1,
        grid=(n_req // group, group * n_pages),
        in_specs=[pl.BlockSpec((None, PAGE, LANES), lambda g, s, pt: (pt[g * (group * n_pages) + s], 0, 0)),
                  pl.BlockSpec((None, PAGE, LANES), lambda g, s, pt: (pt[g * (group * n_pages) + s], 0, 1)),
                  pl.BlockSpec(pos.shape, lambda g, s, pt: (0, 0, 0)),
                  pl.BlockSpec(w1.shape, lambda g, s, pt: (0, 0, 0)),
                  pl.BlockSpec(w2.shape, lambda g, s, pt: (0, 0))],
        out_specs=pl.BlockSpec((group * rows_req, width), lambda g, s, pt: (g, 0)),
        scratch_shapes=[pltpu.VMEM((CMP_STRIDE, group * rows_req, width), F32)],
    )
    return pl.pallas_call(
        functools.partial(_compress_kernel, n_pages=n_pages),
        grid_spec=grid_spec,
        out_shape=jax.ShapeDtypeStruct((n_req * rows_req, width), F32),
        compiler_params=_cparams(("parallel", "arbitrary")),
        name="nsa_compress",
    )(page_ids, pool, pool, pos, w1, w2)


def _moba_prompt_kernel(slopes_ref, q_ref, k_ref, v_ref, o_ref, kmean_scr, vt_scr, bias_scr, *, n_blocks):
    pr = pl.program_id(1)
    t = pl.program_id(2)
    tq = MOBA_BLOCK

    @pl.when(t == 0)
    def _():
        for n in range(n_blocks):
            kmean_scr[n:n + 1, :] = jnp.mean(k_ref[n * tq:(n + 1) * tq, :], axis=0, keepdims=True)
            vt_scr[:, n * tq:(n + 1) * tq] = v_ref[n * tq:(n + 1) * tq, :].T.astype(BF16)

    q = q_ref[...]
    lane = lax.broadcasted_iota(jnp.int32, (1, LANES), 1)
    krow = lax.broadcasted_iota(jnp.int32, (tq, tq), 0)
    qcol = lax.broadcasted_iota(jnp.int32, (tq, tq), 1)
    d0 = (qcol - krow).astype(F32)
    blk = lax.broadcasted_iota(jnp.int32, (n_blocks, tq), 0)
    past = blk < t
    t0 = pl.multiple_of(t * tq, tq)
    k_own = k_ref[pl.ds(t0, tq), :].astype(BF16)
    outs = []
    for hh in range(2):
        head_lanes = (lane < HEAD_DIM) if hh == 0 else (lane >= HEAD_DIM)
        qf = jnp.where(head_lanes, q, 0.0)
        qb = (qf * SCALE).astype(BF16)
        slope = slopes_ref[2 * pr + hh]
        gate = jnp.where(past, _nt_dot(kmean_scr[...], qf, lax.Precision.HIGHEST), NEG)
        rank = jnp.zeros((n_blocks, tq), jnp.int32)
        for m in range(n_blocks):
            gm = gate[m:m + 1, :]
            rank = rank + ((gm > gate) | ((gm == gate) & (m < blk))).astype(jnp.int32)
        bias_scr[...] = jnp.where(past & (rank < MOBA_TOPK), 0.0, NEG)

        s = _nt_dot(k_own, qb) - slope * d0
        s = jnp.where(krow <= qcol, s, NEG)
        m0 = jnp.max(s, axis=0, keepdims=True)
        p = jnp.exp(s - m0)
        l0 = jnp.sum(p, axis=0, keepdims=True)
        acc0 = jnp.dot(vt_scr[hh * HEAD_DIM:(hh + 1) * HEAD_DIM, pl.ds(t0, tq)], p.astype(BF16),
                       preferred_element_type=F32)

        def body(j, carry):
            m_i, l_i, acc = carry
            j0 = pl.multiple_of(j * tq, tq)
            kj = k_ref[pl.ds(j0, tq), :].astype(BF16)
            dist = d0 + ((t - j) * tq).astype(F32)
            sj = _nt_dot(kj, qb) - slope * dist + bias_scr[pl.ds(j, 1), :]
            m_new = jnp.maximum(m_i, jnp.max(sj, axis=0, keepdims=True))
            alpha = jnp.exp(m_i - m_new)
            pj = jnp.exp(sj - m_new)
            l_new = alpha * l_i + jnp.sum(pj, axis=0, keepdims=True)
            acc = alpha * acc + jnp.dot(vt_scr[hh * HEAD_DIM:(hh + 1) * HEAD_DIM, pl.ds(j0, tq)], pj.astype(BF16),
                                        preferred_element_type=F32)
            return m_new, l_new, acc

        _, l_f, acc_f = lax.fori_loop(0, t, body, (m0, l0, acc0))
        outs.append(acc_f / l_f)
    o_ref[...] = jnp.concatenate(outs, axis=0).T


def _moba_prompt(q, kv, slopes, batch, seq):
    n_blocks = seq // MOBA_BLOCK
    pairs = D_MOBA // LANES
    return pl.pallas_call(
        functools.partial(_moba_prompt_kernel, n_blocks=n_blocks),
        grid=(batch, pairs, n_blocks),
        in_specs=[pl.BlockSpec(memory_space=pltpu.SMEM),
                  pl.BlockSpec((MOBA_BLOCK, LANES), lambda b, p, t: (b * n_blocks + t, p)),
                  pl.BlockSpec((seq, LANES), lambda b, p, t: (b, p)),
                  pl.BlockSpec((seq, LANES), lambda b, p, t: (b, pairs + p))],
        out_specs=pl.BlockSpec((MOBA_BLOCK, LANES), lambda b, p, t: (b * n_blocks + t, p)),
        out_shape=jax.ShapeDtypeStruct((batch * seq, D_MOBA), F32),
        scratch_shapes=[pltpu.VMEM((n_blocks, LANES), F32),
                        pltpu.VMEM((LANES, seq), BF16),
                        pltpu.VMEM((n_blocks, MOBA_BLOCK), F32)],
        compiler_params=_cparams(("parallel", "parallel", "arbitrary")),
        name="moba_prompt",
    )(slopes, q, kv, kv)


def _cumsum_after(strict_upper, lgt):
    hi = lgt.astype(BF16)
    lo = (lgt - hi.astype(F32)).astype(BF16)
    return (jnp.dot(strict_upper, hi, preferred_element_type=F32)
            + jnp.dot(strict_upper, lo, preferred_element_type=F32))


def _sb_prompt_kernel(q_ref, k_ref, v_ref, o_ref, vt_scr, *, n_tiles, tq):
    t = pl.program_id(2)

    @pl.when(t == 0)
    def _():
        for n in range(n_tiles):
            vt_scr[:, n * tq:(n + 1) * tq] = v_ref[n * tq:(n + 1) * tq, :].T.astype(BF16)

    q = q_ref[...]
    lane = lax.broadcasted_iota(jnp.int32, (1, LANES), 1)
    krow = lax.broadcasted_iota(jnp.int32, (tq, tq), 0)
    qcol = lax.broadcasted_iota(jnp.int32, (tq, tq), 1)
    causal = krow < qcol
    upper = (qcol > krow).astype(BF16)
    t0 = pl.multiple_of(t * tq, tq)
    k_own = k_ref[pl.ds(t0, tq), :].astype(BF16)
    outs = []
    for hh in range(2):
        head_lanes = (lane < HEAD_DIM) if hh == 0 else (lane >= HEAD_DIM)
        qb = (jnp.where(head_lanes, q, 0.0) * SCALE).astype(BF16)
        z = _nt_dot(k_own, qb)
        sp = _softplus(z)
        lgt = jnp.where(causal, -sp, 0.0)
        a = jnp.where(causal, jnp.exp(z - sp + _cumsum_after(upper, lgt)), 0.0)
        acc0 = jnp.dot(vt_scr[hh * HEAD_DIM:(hh + 1) * HEAD_DIM, pl.ds(t0, tq)], a.astype(BF16),
                       preferred_element_type=F32)
        c0 = jnp.sum(lgt, axis=0, keepdims=True)

        def body(i, carry):
            c, acc = carry
            j0 = pl.multiple_of((t - 1 - i) * tq, tq)
            kj = k_ref[pl.ds(j0, tq), :].astype(BF16)
            zj = _nt_dot(kj, qb)
            spj = _softplus(zj)
            aj = jnp.exp(zj - spj + (_cumsum_after(upper, -spj) + c))
            acc = acc + jnp.dot(vt_scr[hh * HEAD_DIM:(hh + 1) * HEAD_DIM, pl.ds(j0, tq)], aj.astype(BF16),
                                preferred_element_type=F32)
            return c - jnp.sum(spj, axis=0, keepdims=True), acc

        _, acc_f = lax.fori_loop(0, t, body, (c0, acc0))
        outs.append(acc_f)
    o_ref[...] = jnp.concatenate(outs, axis=0).T


def _sb_prompt(q, kv, batch, seq):
    tq = 256
    n_tiles = seq // tq
    pairs = D_SB // LANES
    return pl.pallas_call(
        functools.partial(_sb_prompt_kernel, n_tiles=n_tiles, tq=tq),
        grid=(batch, pairs, n_tiles),
        in_specs=[pl.BlockSpec((tq, LANES), lambda b, p, t: (b * n_tiles + t, p)),
                  pl.BlockSpec((seq, LANES), lambda b, p, t: (b, p)),
                  pl.BlockSpec((seq, LANES), lambda b, p, t: (b, pairs + p))],
        out_specs=pl.BlockSpec((tq, LANES), lambda b, p, t: (b * n_tiles + t, p)),
        out_shape=jax.ShapeDtypeStruct((batch * seq, D_SB), F32),
        scratch_shapes=[pltpu.VMEM((LANES, seq), BF16)],
        compiler_params=_cparams(("parallel", "parallel", "arbitrary")),
        name="sb_prompt",
    )(q, kv, kv)


def _nsa_prompt_kernel(slopes_ref, q_ref, gates_ref, cmp_ref, sel_ref, win_ref, ovl_ref, o_ref,
                       vcmp_scr, vsel_scr, vwin_scr, bias_scr, *, seq):
    t = pl.program_id(1)
    tq = LANES
    n_tiles = seq // tq
    n_rows_cmp = seq // CMP_STRIDE
    n_cmp = (seq - CMP_LEN) // CMP_STRIDE + 1
    n_sel_blocks = seq // SEL_BLOCK
    n_top = min(SEL_TOPK, n_sel_blocks)
    wide = GROUP * tq

    @pl.when(t == 0)
    def _():
        vcmp_scr[...] = cmp_ref[:, LANES:2 * LANES].T.astype(BF16)
        for n in range(n_tiles):
            vsel_scr[:, n * tq:(n + 1) * tq] = sel_ref[n * tq:(n + 1) * tq, LANES:2 * LANES].T.astype(BF16)
            vwin_scr[:, n * tq:(n + 1) * tq] = win_ref[n * tq:(n + 1) * tq, LANES:2 * LANES].T.astype(BF16)

    lane = lax.broadcasted_iota(jnp.int32, (1, LANES), 1)
    qi = lax.broadcasted_iota(jnp.int32, (1, wide), 1) % tq
    qpos = t * tq + qi
    krow = lax.broadcasted_iota(jnp.int32, (tq, wide), 0)
    d0 = qi - krow
    q = q_ref[...]
    gates_t = jax.nn.sigmoid(gates_ref[...]).T
    t0 = pl.multiple_of(t * tq, tq)

    out_t = []
    for g in range(KV_NSA):
        glanes = (lane < HEAD_DIM) if g == 0 else (lane >= HEAD_DIM)
        qg = jnp.concatenate([jnp.where(glanes, q[:, r * LANES:(r + 1) * LANES], 0.0) for r in range(GROUP)], axis=0)
        qb = (qg * SCALE).astype(BF16)
        slope = jnp.concatenate([jnp.full((1, tq), slopes_ref[g * GROUP + r], F32) for r in range(GROUP)], axis=1)
        vrows = slice(g * HEAD_DIM, (g + 1) * HEAD_DIM)

        nrow = lax.broadcasted_iota(jnp.int32, (n_rows_cmp, wide), 0)
        dist_c = qpos - (nrow * CMP_STRIDE + (CMP_LEN - 1))
        vis_c = (dist_c >= 0) & (nrow < n_cmp)
        sc = _nt_dot(cmp_ref[:, 0:LANES].astype(BF16), qb) - slope * dist_c.astype(F32)
        sc = jnp.where(vis_c, sc, NEG)
        pc = jnp.where(vis_c, jnp.exp(sc - jnp.max(sc, axis=0, keepdims=True)), 0.0)
        lc = jnp.sum(pc, axis=0, keepdims=True)
        pc = pc / jnp.where(lc > 0.0, lc, 1.0)
        o_cmp = jnp.dot(vcmp_scr[vrows, :], pc.astype(BF16), preferred_element_type=F32)

        psum = pc[:, 0:tq]
        for r in range(1, GROUP):
            psum = psum + pc[:, r * tq:(r + 1) * tq]
        imp = jnp.dot(ovl_ref[...], psum, preferred_element_type=F32, precision=lax.Precision.HIGHEST)
        blk = lax.broadcasted_iota(jnp.int32, (n_sel_blocks, tq), 0)
        own = qpos[:, 0:tq] // SEL_BLOCK
        cand = blk <= own
        score = jnp.where((blk == own) | (blk == 0), BIG, jnp.where(cand, imp, -BIG))
        rank = jnp.zeros((n_sel_blocks, tq), jnp.int32)
        for m in range(n_sel_blocks):
            sm = score[m:m + 1, :]
            rank = rank + ((sm > score) | ((sm == score) & (m < blk))).astype(jnp.int32)
        bias_scr[...] = jnp.where(cand & (rank < n_top), 0.0, NEG)

        def sel_body(j, carry):
            m_i, l_i, acc = carry
            j0 = pl.multiple_of(j * tq, tq)
            kj = sel_ref[pl.ds(j0, tq), 0:LANES].astype(BF16)
            b0 = bias_scr[pl.ds(2 * j, 1), :]
            b1 = bias_scr[pl.ds(2 * j + 1, 1), :]
            bias = jnp.where(krow[:, 0:tq] < SEL_BLOCK, b0, b1)
            bias = jnp.concatenate([bias] * GROUP, axis=1)
            dist = d0 + (t - j) * tq
            sj = _nt_dot(kj, qb) - slope * dist.astype(F32) + bias
            sj = jnp.where(dist >= 0, sj, NEG)
            m_new = jnp.maximum(m_i, jnp.max(sj, axis=0, keepdims=True))
            alpha = jnp.exp(m_i - m_new)
            pj = jnp.exp(sj - m_new)
            l_new = alpha * l_i + jnp.sum(pj, axis=0, keepdims=True)
            acc = alpha * acc + jnp.dot(vsel_scr[vrows, pl.ds(j0, tq)], pj.astype(BF16), preferred_element_type=F32)
            return m_new, l_new, acc

        init = (jnp.full((1, wide), NEG, F32), jnp.zeros((1, wide), F32), jnp.zeros((HEAD_DIM, wide), F32))
        _, l_s, acc_s = lax.fori_loop(0, t + 1, sel_body, init)
        o_sel = acc_s / l_s

        def win_tile(j, carry):
            m_i, l_i, acc = carry
            j0 = pl.multiple_of(j * tq, tq)
            kj = win_ref[pl.ds(j0, tq), 0:LANES].astype(BF16)
            dist = d0 + (t - j) * tq
            sj = _nt_dot(kj, qb) - slope * dist.astype(F32)
            sj = jnp.where((dist >= 0) & (dist < WINDOW), sj, NEG)
            m_new = jnp.maximum(m_i, jnp.max(sj, axis=0, keepdims=True))
            alpha = jnp.exp(m_i - m_new)
            pj = jnp.exp(sj - m_new)
            l_new = alpha * l_i + jnp.sum(pj, axis=0, keepdims=True)
            acc = alpha * acc + jnp.dot(vwin_scr[vrows, pl.ds(j0, tq)], pj.astype(BF16), preferred_element_type=F32)
            return m_new, l_new, acc

        carry = win_tile(t, init)
        _, l_w, acc_w = lax.fori_loop(0, jnp.minimum(t, WINDOW // tq), lambda i, c: win_tile(t - 1 - i, c), carry)
        o_win = acc_w / l_w

        cols = []
        for r in range(GROUP):
            h = g * GROUP + r
            sl = slice(r * tq, (r + 1) * tq)
            cols.append(gates_t[h:h + 1, :] * o_cmp[:, sl] + gates_t[H_NSA + h:H_NSA + h + 1, :] * o_sel[:, sl]
                        + gates_t[2 * H_NSA + h:2 * H_NSA + h + 1, :] * o_win[:, sl])
        out_t.append(cols)

    for r in range(GROUP):
        o_ref[:, r * LANES:(r + 1) * LANES] = jnp.concatenate([out_t[0][r], out_t[1][r]], axis=0).T


def _nsa_prompt(q, gates, kcv, kv_sel, kv_win, overlap_t, slopes, batch, seq):
    tq = LANES
    n_tiles = seq // tq
    n_rows_cmp = seq // CMP_STRIDE
    return pl.pallas_call(
        functools.partial(_nsa_prompt_kernel, seq=seq),
        grid=(batch, n_tiles),
        in_specs=[pl.BlockSpec(memory_space=pltpu.SMEM),
                  pl.BlockSpec((tq, D_NSA), lambda b, t: (b * n_tiles + t, 0)),
                  pl.BlockSpec((tq, LANES), lambda b, t: (b * n_tiles + t, 0)),
                  pl.BlockSpec((n_rows_cmp, 2 * LANES), lambda b, t: (b, 0)),
                  pl.BlockSpec((seq, 2 * LANES), lambda b, t: (b, 0)),
                  pl.BlockSpec((seq, 2 * LANES), lambda b, t: (b, 0)),
                  pl.BlockSpec(overlap_t.shape, lambda b, t: (0, 0))],
        out_specs=pl.BlockSpec((tq, D_NSA), lambda b, t: (b * n_tiles + t, 0)),
        out_shape=jax.ShapeDtypeStruct((batch * seq, D_NSA), F32),
        scratch_shapes=[pltpu.VMEM((LANES, n_rows_cmp), BF16),
                        pltpu.VMEM((LANES, seq), BF16),
                        pltpu.VMEM((LANES, seq), BF16),
                        pltpu.VMEM((seq // SEL_BLOCK, tq), F32)],
        compiler_params=_cparams(("parallel", "arbitrary")),
        name="nsa_prompt",
    )(slopes, q, gates, kcv, kv_sel, kv_win, overlap_t)


def _head_rows(q8, n_heads, lane_of_head):
    lanes = lax.broadcasted_iota(jnp.int32, (1, q8.shape[1]), 1)
    rows = []
    for h in range(n_heads):
        lo = lane_of_head(h)
        rows.append(jnp.where((lanes >= lo) & (lanes < lo + HEAD_DIM), q8, 0.0))
    return jnp.concatenate(rows, axis=0)


def _row_const(values, n_rows):
    row = lax.broadcasted_iota(jnp.int32, (n_rows, 1), 0) // Q_ROWS
    out = jnp.zeros((n_rows, 1), F32)
    for h, v in enumerate(values):
        out = jnp.where(row == h, v, out)
    return out


def _fold_heads(o, n_heads, lane_of_head):
    lanes = lax.broadcasted_iota(jnp.int32, (1, o.shape[1]), 1)
    out = jnp.zeros((Q_ROWS, o.shape[1]), F32)
    for h in range(n_heads):
        lo = lane_of_head(h)
        out = out + jnp.where((lanes >= lo) & (lanes < lo + HEAD_DIM), o[h * Q_ROWS:(h + 1) * Q_ROWS, :], 0.0)
    return out


def _moba_decode_kernel(pt_ref, slopes_ref, q_ref, new_ref, page_ref, o_ref,
                        qb_scr, slope_scr, new_scr, m_scr, l_scr, acc_scr, ksum_scr, *, n_pages, past):
    del pt_ref
    p = pl.program_id(1)
    rows = H_MOBA * Q_ROWS
    lane_of_head = lambda h: h * HEAD_DIM
    qoff = lax.broadcasted_iota(jnp.int32, (rows, 1), 0) % Q_ROWS
    key = lax.broadcasted_iota(jnp.int32, (rows, PAGE), 1)

    def page_partial(k, v, slot, dist, masked):
        s = _nt_dot(qb_scr[...], k.astype(BF16)) - slope_scr[...] * dist.astype(F32)
        if masked:
            s = jnp.where(dist >= 0, s, NEG)
        m = jnp.max(s, axis=1, keepdims=True)
        pe = jnp.exp(s - m)
        m_scr[slot] = jnp.broadcast_to(m, (rows, LANES))
        l_scr[slot] = jnp.broadcast_to(jnp.sum(pe, axis=1, keepdims=True), (rows, LANES))
        acc_scr[slot] = jnp.dot(pe.astype(BF16), v.astype(BF16), preferred_element_type=F32)

    @pl.when(p == 0)
    def _():
        qb_scr[...] = (_head_rows(q_ref[...], H_MOBA, lane_of_head) * SCALE).astype(BF16)
        slope_scr[...] = jnp.broadcast_to(_row_const([slopes_ref[h] for h in range(H_MOBA)], rows), (rows, PAGE))
        new_scr[...] = jnp.zeros(new_scr.shape, F32)
        new_scr[0:NEW_ROWS, :] = new_ref[...]
        page_partial(new_scr[:, 0:D_MOBA], new_scr[:, D_MOBA:2 * D_MOBA], n_pages, qoff - key, True)

    pages_per_block = MOBA_BLOCK // PAGE
    kpage = page_ref[:, 0:D_MOBA]
    ksum_scr[p % pages_per_block, pl.ds(p // pages_per_block, 1), :] = jnp.sum(kpage, axis=0, keepdims=True)
    page_partial(kpage, page_ref[:, D_MOBA:2 * D_MOBA], p, (past + qoff) - (p * PAGE + key), False)

    @pl.when(p == n_pages - 1)
    def _():
        n_blocks = n_pages // pages_per_block
        ksum = ksum_scr[0]
        for i in range(1, pages_per_block):
            ksum = ksum + ksum_scr[i]
        kmean = ksum * (1.0 / MOBA_BLOCK)
        qf = _head_rows(q_ref[...], H_MOBA, lane_of_head)
        gate = _nt_dot(qf, kmean, lax.Precision.HIGHEST)
        blk = lax.broadcasted_iota(jnp.int32, (rows, n_blocks), 1)
        rank = jnp.zeros((rows, n_blocks), jnp.int32)
        for m in range(n_blocks):
            gm = gate[:, m:m + 1]
            rank = rank + ((gm > gate) | ((gm == gate) & (m < blk))).astype(jnp.int32)
        chosen = rank < MOBA_TOPK
        m_tot = m_scr[n_pages][:, 0:1]
        for n in range(n_blocks):
            for i in range(pages_per_block):
                m_tot = jnp.maximum(m_tot, jnp.where(chosen[:, n:n + 1], m_scr[n * pages_per_block + i][:, 0:1], NEG))
        w = jnp.exp(m_scr[n_pages][:, 0:1] - m_tot)
        num = w * acc_scr[n_pages]
        den = w * l_scr[n_pages][:, 0:1]
        for n in range(n_blocks):
            for i in range(pages_per_block):
                slot = n * pages_per_block + i
                w = jnp.where(chosen[:, n:n + 1], jnp.exp(m_scr[slot][:, 0:1] - m_tot), 0.0)
                num = num + w * acc_scr[slot]
                den = den + w * l_scr[slot][:, 0:1]
        o_ref[...] = _fold_heads(num / den, H_MOBA, lane_of_head)


def _moba_decode(q8, new16, pool, page_ids, slopes, n_req, n_pages):
    rows = H_MOBA * Q_ROWS
    past = n_pages * PAGE
    grid_spec = pltpu.PrefetchScalarGridSpec(
        num_scalar_prefetch=1,
        grid=(n_req, n_pages),
        in_specs=[pl.BlockSpec(memory_space=pltpu.SMEM),
                  pl.BlockSpec((None, Q_ROWS, D_MOBA), lambda b, p, pt: (b, 0, 0)),
                  pl.BlockSpec((None, NEW_ROWS, 2 * D_MOBA), lambda b, p, pt: (b, 0, 0)),
                  pl.BlockSpec((None, PAGE, 2 * D_MOBA), lambda b, p, pt: (pt[b * n_pages + p], 0, 0))],
        out_specs=pl.BlockSpec((None, Q_ROWS, D_MOBA), lambda b, p, pt: (b, 0, 0)),
        scratch_shapes=[pltpu.VMEM((rows, D_MOBA), BF16),
                        pltpu.VMEM((rows, PAGE), F32),
                        pltpu.VMEM((PAGE, 2 * D_MOBA), F32),
                        pltpu.VMEM((n_pages + 1, rows, LANES), F32),
                        pltpu.VMEM((n_pages + 1, rows, LANES), F32),
                        pltpu.VMEM((n_pages + 1, rows, D_MOBA), F32),
                        pltpu.VMEM((MOBA_BLOCK // PAGE, n_pages * PAGE // MOBA_BLOCK, D_MOBA), F32)],
    )
    return pl.pallas_call(
        functools.partial(_moba_decode_kernel, n_pages=n_pages, past=past),
        grid_spec=grid_spec,
        out_shape=jax.ShapeDtypeStruct((n_req, Q_ROWS, D_MOBA), F32),
        compiler_params=_cparams(("parallel", "arbitrary")),
        name="moba_decode",
    )(page_ids, slopes, q8, new16, pool)


def _sb_decode_kernel(pt_ref, q_ref, new_ref, page_ref, o_ref, qb_scr, new_scr, c_scr, acc_scr, *, n_pages, past):
    del pt_ref
    p = pl.program_id(1)
    rows = H_SB * Q_ROWS
    lane_of_head = lambda h: h * HEAD_DIM
    qoff = lax.broadcasted_iota(jnp.int32, (rows, 1), 0) % Q_ROWS
    key = lax.broadcasted_iota(jnp.int32, (rows, PAGE), 1)
    kj = lax.broadcasted_iota(jnp.int32, (PAGE, PAGE), 0)
    ks = lax.broadcasted_iota(jnp.int32, (PAGE, PAGE), 1)
    after_mat = (kj > ks).astype(BF16)

    def page_update(k, v, causal):
        z = _nt_dot(qb_scr[...], k.astype(BF16))
        sp = _softplus(z)
        lgt = -sp if causal is None else jnp.where(causal, -sp, 0.0)
        hi = lgt.astype(BF16)
        lo = (lgt - hi.astype(F32)).astype(BF16)
        after = (jnp.dot(hi, after_mat, preferred_element_type=F32)
                 + jnp.dot(lo, after_mat, preferred_element_type=F32)) + c_scr[...]
        a = jnp.exp(z - sp + after)
        if causal is not None:
            a = jnp.where(causal, a, 0.0)
        acc_scr[...] = acc_scr[...] + jnp.dot(a.astype(BF16), v.astype(BF16), preferred_element_type=F32)
        c_scr[...] = c_scr[...] + jnp.broadcast_to(jnp.sum(lgt, axis=1, keepdims=True), (rows, PAGE))

    @pl.when(p == 0)
    def _():
        qb_scr[...] = (_head_rows(q_ref[...], H_SB, lane_of_head) * SCALE).astype(BF16)
        new_scr[...] = jnp.zeros(new_scr.shape, F32)
        new_scr[0:NEW_ROWS, :] = new_ref[...]
        c_scr[...] = jnp.zeros(c_scr.shape, F32)
        acc_scr[...] = jnp.zeros(acc_scr.shape, F32)
        page_update(new_scr[:, 0:D_SB], new_scr[:, D_SB:2 * D_SB], key < qoff)

    page_update(page_ref[:, 0:D_SB], page_ref[:, D_SB:2 * D_SB], None)

    @pl.when(p == n_pages - 1)
    def _():
        o_ref[...] = _fold_heads(acc_scr[...], H_SB, lane_of_head)


def _sb_decode(q8, new16, pool, page_ids, n_req, n_pages):
    rows = H_SB * Q_ROWS
    grid_spec = pltpu.PrefetchScalarGridSpec(
        num_scalar_prefetch=1,
        grid=(n_req, n_pages),
        in_specs=[pl.BlockSpec((None, Q_ROWS, D_SB), lambda b, p, pt: (b, 0, 0)),
                  pl.BlockSpec((None, NEW_ROWS, 2 * D_SB), lambda b, p, pt: (b, 0, 0)),
                  pl.BlockSpec((None, PAGE, 2 * D_SB), lambda b, p, pt: (pt[b * n_pages + (n_pages - 1 - p)], 0, 0))],
        out_specs=pl.BlockSpec((None, Q_ROWS, D_SB), lambda b, p, pt: (b, 0, 0)),
        scratch_shapes=[pltpu.VMEM((rows, D_SB), BF16),
                        pltpu.VMEM((PAGE, 2 * D_SB), F32),
                        pltpu.VMEM((rows, PAGE), F32),
                        pltpu.VMEM((rows, D_SB), F32)],
    )
    return pl.pallas_call(
        functools.partial(_sb_decode_kernel, n_pages=n_pages, past=n_pages * PAGE),
        grid_spec=grid_spec,
        out_shape=jax.ShapeDtypeStruct((n_req, Q_ROWS, D_SB), F32),
        compiler_params=_cparams(("parallel", "arbitrary")),
        name="sb_decode",
    )(page_ids, q8, new16, pool)


def _nsa_decode_kernel(pt_ref, slopes_ref, q_ref, gates_ref, cmp_ref, selnew_ref, winbuf_ref, winnew_ref, ovl_ref,
                       page_ref, o_ref, qb_scr, slope_scr, new_scr, bias_scr, ocmp_scr, owin_scr,
                       m_scr, l_scr, acc_scr, *, n_pages, past):
    del pt_ref
    p = pl.program_id(1)
    rows = H_NSA * Q_ROWS
    grows = KV_NSA * Q_ROWS
    n_rows_cmp = cmp_ref.shape[0]
    total = past + Q_ROWS // 2
    n_cmp = (total - CMP_LEN) // CMP_STRIDE + 1
    n_sel_blocks = -(-total // SEL_BLOCK)
    n_top = min(SEL_TOPK, n_sel_blocks)
    win_buf = winbuf_ref.shape[0]
    lane_of_head = lambda h: (h // GROUP) * HEAD_DIM
    qoff = lax.broadcasted_iota(jnp.int32, (rows, 1), 0) % Q_ROWS
    key = lax.broadcasted_iota(jnp.int32, (rows, PAGE), 1)

    def softmax_update(s, v, first):
        m_new = jnp.max(s, axis=1, keepdims=True)
        if not first:
            m_new = jnp.maximum(m_scr[...][:, 0:1], m_new)
        pe = jnp.exp(s - m_new)
        l_new = jnp.sum(pe, axis=1, keepdims=True)
        acc = jnp.dot(pe.astype(BF16), v.astype(BF16), preferred_element_type=F32)
        if not first:
            alpha = jnp.exp(m_scr[...][:, 0:1] - m_new)
            l_new = alpha * l_scr[...][:, 0:1] + l_new
            acc = alpha * acc_scr[...] + acc
        m_scr[...] = jnp.broadcast_to(m_new, (rows, LANES))
        l_scr[...] = jnp.broadcast_to(l_new, (rows, LANES))
        acc_scr[...] = acc

    @pl.when(p == 0)
    def _():
        q = q_ref[...]
        qh = jnp.concatenate([q[:, (h % GROUP) * LANES:(h % GROUP + 1) * LANES] for h in range(H_NSA)], axis=0)
        lanes = lax.broadcasted_iota(jnp.int32, (rows, LANES), 1)
        grp = lax.broadcasted_iota(jnp.int32, (rows, LANES), 0) // (GROUP * Q_ROWS)
        qh = jnp.where((lanes // HEAD_DIM) == grp, qh, 0.0)
        qb = (qh * SCALE).astype(BF16)
        qb_scr[...] = qb
        slope = _row_const([slopes_ref[h] for h in range(H_NSA)], rows)
        slope_scr[...] = jnp.broadcast_to(slope, (rows, PAGE))

        ncol = lax.broadcasted_iota(jnp.int32, (rows, n_rows_cmp), 1)
        dist_c = (past + qoff) - (ncol * CMP_STRIDE + (CMP_LEN - 1))
        vis_c = (dist_c >= 0) & (ncol < n_cmp)
        sc = _nt_dot(qb, cmp_ref[:, 0:LANES].astype(BF16)) - slope * dist_c.astype(F32)
        sc = jnp.where(vis_c, sc, NEG)
        pc = jnp.where(vis_c, jnp.exp(sc - jnp.max(sc, axis=1, keepdims=True)), 0.0)
        lc = jnp.sum(pc, axis=1, keepdims=True)
        pc = pc / jnp.where(lc > 0.0, lc, 1.0)
        ocmp_scr[...] = jnp.dot(pc.astype(BF16), cmp_ref[:, LANES:2 * LANES].astype(BF16), preferred_element_type=F32)

        psum = []
        for g in range(KV_NSA):
            acc = pc[g * GROUP * Q_ROWS:(g * GROUP + 1) * Q_ROWS, :]
            for r in range(1, GROUP):
                acc = acc + pc[(g * GROUP + r) * Q_ROWS:(g * GROUP + r + 1) * Q_ROWS, :]
            psum.append(acc)
        imp = jnp.dot(jnp.concatenate(psum, axis=0), ovl_ref[...], preferred_element_type=F32,
                      precision=lax.Precision.HIGHEST)
        blk = lax.broadcasted_iota(jnp.int32, (grows, LANES), 1)
        own = (past + lax.broadcasted_iota(jnp.int32, (grows, 1), 0) % Q_ROWS) // SEL_BLOCK
        cand = blk <= own
        score = jnp.where((blk == own) | (blk == 0), BIG, jnp.where(cand, imp, -BIG))
        rank = jnp.zeros((grows, LANES), jnp.int32)
        for m in range(n_sel_blocks):
            sm = score[:, m:m + 1]
            rank = rank + ((sm > score) | ((sm == score) & (m < blk))).astype(jnp.int32)
        bias_g = jnp.where(cand & (rank < n_top), 0.0, NEG)
        bias_scr[...] = jnp.concatenate([bias_g[(h // GROUP) * Q_ROWS:(h // GROUP + 1) * Q_ROWS, :]
                                         for h in range(H_NSA)], axis=0)

        new_scr[...] = jnp.zeros(new_scr.shape, F32)
        new_scr[0:NEW_ROWS, :] = winnew_ref[...]
        wkey = lax.broadcasted_iota(jnp.int32, (rows, win_buf), 1)
        dist_b = qoff + (win_buf - wkey)
        sb = _nt_dot(qb, winbuf_ref[:, 0:LANES].astype(BF16)) - slope * dist_b.astype(F32)
        sb = jnp.where((dist_b < WINDOW) & (past - win_buf + wkey >= 0), sb, NEG)
        dist_n = qoff - key
        sn = _nt_dot(qb, new_scr[:, 0:LANES].astype(BF16)) - slope * dist_n.astype(F32)
        sn = jnp.where(dist_n >= 0, sn, NEG)
        mw = jnp.maximum(jnp.max(sb, axis=1, keepdims=True), jnp.max(sn, axis=1, keepdims=True))
        pb = jnp.exp(sb - mw)
        pn = jnp.exp(sn - mw)
        lw = jnp.sum(pb, axis=1, keepdims=True) + jnp.sum(pn, axis=1, keepdims=True)
        ow = (jnp.dot(pb.astype(BF16), winbuf_ref[:, LANES:2 * LANES].astype(BF16), preferred_element_type=F32)
              + jnp.dot(pn.astype(BF16), new_scr[:, LANES:2 * LANES].astype(BF16), preferred_element_type=F32))
        owin_scr[...] = ow / lw

        new_scr[0:NEW_ROWS, :] = selnew_ref[...]
        ss = _nt_dot(qb, new_scr[:, 0:LANES].astype(BF16)) - slope * dist_n.astype(F32)
        softmax_update(jnp.where(dist_n >= 0, ss, NEG), new_scr[:, LANES:2 * LANES], True)

    lanes_b = lax.broadcasted_iota(jnp.int32, (rows, LANES), 1)
    blocks_per_page = PAGE // SEL_BLOCK
    bias = jnp.zeros((rows, PAGE), F32)
    for i in range(blocks_per_page):
        col = jnp.sum(jnp.where(lanes_b == p * blocks_per_page + i, bias_scr[...], 0.0), axis=1, keepdims=True)
        bias = jnp.where((key // SEL_BLOCK) == i, col, bias)
    dist = (past + qoff) - (p * PAGE + key)
    s = _nt_dot(qb_scr[...], page_ref[:, 0:LANES].astype(BF16)) - slope_scr[...] * dist.astype(F32) + bias
    softmax_update(s, page_ref[:, LANES:2 * LANES], False)

    @pl.when(p == n_pages - 1)
    def _():
        o_sel = acc_scr[...] / l_scr[...][:, 0:1]
        gts = jax.nn.sigmoid(gates_ref[...])
        lane = lax.broadcasted_iota(jnp.int32, (1, LANES), 1)
        heads = []
        for h in range(H_NSA):
            sl = slice(h * Q_ROWS, (h + 1) * Q_ROWS)
            heads.append(gts[:, h:h + 1] * ocmp_scr[sl, :] + gts[:, H_NSA + h:H_NSA + h + 1] * o_sel[sl, :]
                         + gts[:, 2 * H_NSA + h:2 * H_NSA + h + 1] * owin_scr[sl, :])
        for r in range(GROUP):
            o_ref[:, r * LANES:(r + 1) * LANES] = jnp.where(lane < HEAD_DIM, heads[r], heads[GROUP + r])


def _nsa_decode(q8, gates8, kcv, selnew16, winbuf, winnew16, overlap, pool, page_ids, slopes, n_req, n_pages):
    rows = H_NSA * Q_ROWS
    n_rows_cmp = n_pages * (PAGE // CMP_STRIDE)
    win_buf = winbuf.shape[1]
    per_req = lambda shape: pl.BlockSpec((None,) + shape, lambda b, p, pt: (b, 0, 0))
    grid_spec = pltpu.PrefetchScalarGridSpec(
        num_scalar_prefetch=1,
        grid=(n_req, n_pages),
        in_specs=[pl.BlockSpec(memory_space=pltpu.SMEM),
                  per_req((Q_ROWS, D_NSA)),
                  per_req((Q_ROWS, LANES)),
                  pl.BlockSpec((n_rows_cmp, 2 * LANES), lambda b, p, pt: (b, 0)),
                  per_req((NEW_ROWS, 2 * LANES)),
                  per_req((win_buf, 2 * LANES)),
                  per_req((NEW_ROWS, 2 * LANES)),
                  pl.BlockSpec(overlap.shape, lambda b, p, pt: (0, 0)),
                  pl.BlockSpec((None, PAGE, 2 * LANES), lambda b, p, pt: (pt[b * n_pages + p], 0, 0))],
        out_specs=per_req((Q_ROWS, D_NSA)),
        scratch_shapes=[pltpu.VMEM((rows, LANES), BF16),
                        pltpu.VMEM((rows, PAGE), F32),
                        pltpu.VMEM((PAGE, 2 * LANES), F32),
                        pltpu.VMEM((rows, LANES), F32),
                        pltpu.VMEM((rows, LANES), F32),
                        pltpu.VMEM((rows, LANES), F32),
                        pltpu.VMEM((rows, LANES), F32),
                        pltpu.VMEM((rows, LANES), F32),
                        pltpu.VMEM((rows, LANES), F32)],
    )
    return pl.pallas_call(
        functools.partial(_nsa_decode_kernel, n_pages=n_pages, past=n_pages * PAGE),
        grid_spec=grid_spec,
        out_shape=jax.ShapeDtypeStruct((n_req, Q_ROWS, D_NSA), F32),
        compiler_params=_cparams(("parallel", "arbitrary")),
        name="nsa_decode",
    )(page_ids, slopes, q8, gates8, kcv, selnew16, winbuf, winnew16, overlap, pool)


def _prep_layer_weights(w_in, cmp_pos, w_ck1, w_ck2, w_cv1, w_cv2, norm_mix, w_out):
    sizes = (D_MOBA, D_MOBA, D_MOBA, D_NSA, D_NSA_KV, D_NSA_KV, D_NSA_KV, D_NSA_KV, D_NSA_KV, D_NSA_KV,
             N_GATES, D_SB, D_SB, D_SB)
    offs = np.concatenate([[0], np.cumsum(sizes)])
    (qa, ka, va, qb, kc, vc, ks, vs, kw, vw, gt, qc, kcs, vcs) = [np.arange(offs[i], offs[i + 1]) for i in range(14)]
    qb_perm = np.concatenate([qb[h * HEAD_DIM:(h + 1) * HEAD_DIM] for h in _NSA_HEAD_ORDER])
    cols = np.concatenate([qa, ka, va, qb_perm, kc, vc, ks, vs, kw, vw, qc, kcs, vcs, gt])
    w = jnp.take(w_in, jnp.asarray(cols), axis=-1)
    w = jnp.pad(w, ((0, 0), (0, 0), (0, LANES - N_GATES))).astype(BF16)

    nsa_perm = np.concatenate([D_MOBA + np.arange(h * HEAD_DIM, (h + 1) * HEAD_DIM) for h in _NSA_HEAD_ORDER])
    mix_perm = jnp.asarray(np.concatenate([np.arange(D_MOBA), nsa_perm, np.arange(D_MOBA + D_NSA, D_MOBA + D_NSA + D_SB)]))
    g_mix = jnp.take(norm_mix, mix_perm, axis=-1)
    wo = jnp.take(w_out, mix_perm, axis=-2).astype(BF16)

    depth = w_in.shape[0]
    k1 = w_ck1.reshape(depth, CMP_LEN, HEAD_DIM, CMP_HID)
    v1 = w_cv1.reshape(depth, CMP_LEN, HEAD_DIM, CMP_HID)
    z1 = jnp.zeros_like(k1)
    w1 = jnp.concatenate([jnp.concatenate([k1, z1, z1, z1], axis=-1), jnp.concatenate([z1, k1, z1, z1], axis=-1),
                          jnp.concatenate([z1, z1, v1, z1], axis=-1), jnp.concatenate([z1, z1, z1, v1], axis=-1)],
                         axis=-2).astype(BF16)
    z2 = jnp.zeros_like(w_ck2)
    w2 = jnp.concatenate([jnp.concatenate([w_ck2, z2, z2, z2], axis=-1), jnp.concatenate([z2, w_ck2, z2, z2], axis=-1),
                          jnp.concatenate([z2, z2, w_cv2, z2], axis=-1), jnp.concatenate([z2, z2, z2, w_cv2], axis=-1)],
                         axis=-2).astype(BF16)
    pos = jnp.concatenate([cmp_pos[:, 0], cmp_pos[:, 0], cmp_pos[:, 1], cmp_pos[:, 1]], axis=-1)[:, :, None, :]
    return w, g_mix, wo, w1, w2, pos


def _overlap(n_rows_cmp, n_cmp, n_blocks, n_cols):
    starts = np.arange(n_rows_cmp) * CMP_STRIDE
    sbs = np.arange(n_cols) * SEL_BLOCK
    ov = (starts[:, None] < sbs[None, :] + SEL_BLOCK) & (starts[:, None] + CMP_LEN > sbs[None, :])
    ov &= (np.arange(n_rows_cmp)[:, None] < n_cmp) & (np.arange(n_cols)[None, :] < n_blocks)
    return ov.astype(np.float32)


def kernel(x_prompt, x_sample, cache_moba_kv, cache_nsa_cmp_kv, cache_nsa_sel_kv, cache_sb_kv, state_nsa_win_kv, page_table, norm_attn, w_in, cmp_pos, w_cmp_k1, w_cmp_k2, w_cmp_v1, w_cmp_v2, norm_mix, w_out, norm_ffn, w_up, w_down, norm_final):
    batch, seq, d_model = x_prompt.shape
    n_req, n_new, _ = x_sample.shape
    depth = w_in.shape[0]
    n_pages = page_table.shape[1]
    past = n_pages * PAGE
    n_phys = cache_moba_kv.shape[1]
    assert n_new * 2 == Q_ROWS and seq % MOBA_BLOCK == 0 and past % MOBA_BLOCK == 0

    slopes_a, slopes_b = _alibi_slopes()
    w_proj, g_mix, w_o, w_c1, w_c2, pos_c = _prep_layer_weights(w_in, cmp_pos, w_cmp_k1, w_cmp_k2, w_cmp_v1, w_cmp_v2,
                                                                norm_mix, w_out)
    w_up_b = w_up.astype(BF16)
    w_down_b = w_down.astype(BF16)
    page_ids = page_table.reshape(-1).astype(jnp.int32)
    prompt_pages = jnp.arange(batch * (seq // PAGE), dtype=jnp.int32)

    n_rows_p = seq // CMP_STRIDE
    ovl_p = jnp.asarray(_overlap(n_rows_p, (seq - CMP_LEN) // CMP_STRIDE + 1, seq // SEL_BLOCK, seq // SEL_BLOCK).T)
    total = past + n_new
    ovl_s = jnp.asarray(_overlap(past // CMP_STRIDE, (total - CMP_LEN) // CMP_STRIDE + 1, -(-total // SEL_BLOCK), LANES))

    xp = x_prompt.reshape(batch * seq, d_model)
    xs = x_sample.reshape(n_req * n_new, d_model)
    pad_q = lambda a: jnp.pad(a.reshape(n_req, n_new, -1), ((0, 0), (0, Q_ROWS - n_new), (0, 0)))
    pad_new = lambda a: jnp.pad(a.reshape(n_req, n_new, -1), ((0, 0), (0, NEW_ROWS - n_new), (0, 0)))
    st_p = [[] for _ in range(5)]
    st_s = [[] for _ in range(5)]
    win_keep = min(WINDOW, seq)
    for l in range(depth):
        (q_moba, kv_moba, q_nsa, kv_cmp, kv_sel, kv_win, q_sb, kv_sb, gates) = _in_proj(xp, norm_attn[l], w_proj[l])
        kcv = _compress(kv_cmp.reshape(batch * (seq // PAGE), PAGE, 2 * D_NSA_KV), prompt_pages, batch, seq // PAGE,
                        pos_c[l], w_c1[l], w_c2[l])
        o_a = _moba_prompt(q_moba, kv_moba, slopes_a, batch, seq)
        o_b = _nsa_prompt(q_nsa, gates, kcv, kv_sel, kv_win, ovl_p, slopes_b, batch, seq)
        o_c = _sb_prompt(q_sb, kv_sb, batch, seq)
        xp = _post(o_a, o_b, o_c, xp, g_mix[l], w_o[l], norm_ffn[l], w_up_b[l], w_down_b[l])
        st_p[0].append(kv_moba.reshape(batch, seq, 2, H_MOBA, HEAD_DIM))
        st_p[1].append(kv_cmp.reshape(batch, seq, 2, KV_NSA, HEAD_DIM))
        st_p[2].append(kv_sel.reshape(batch, seq, 2, KV_NSA, HEAD_DIM))
        st_p[3].append(kv_sb.reshape(batch, seq, 2, H_SB, HEAD_DIM))
        st_p[4].append(kv_win.reshape(batch, seq, 2, KV_NSA, HEAD_DIM)[:, seq - win_keep:])

        (q_moba, kv_moba, q_nsa, kv_cmp, kv_sel, kv_win, q_sb, kv_sb, gates) = _in_proj(xs, norm_attn[l], w_proj[l])
        kcv = _compress(cache_nsa_cmp_kv[l].reshape(n_phys, PAGE, 2 * D_NSA_KV), page_ids, n_req, n_pages,
                        pos_c[l], w_c1[l], w_c2[l])
        o_a = _moba_decode(pad_q(q_moba), pad_new(kv_moba), cache_moba_kv[l].reshape(n_phys, PAGE, 2 * D_MOBA),
                           page_ids, slopes_a, n_req, n_pages)
        win_buf = state_nsa_win_kv[l]
        o_b = _nsa_decode(pad_q(q_nsa), pad_q(gates), kcv, pad_new(kv_sel),
                          win_buf.reshape(n_req, win_buf.shape[1], 2 * D_NSA_KV), pad_new(kv_win), ovl_s,
                          cache_nsa_sel_kv[l].reshape(n_phys, PAGE, 2 * D_NSA_KV), page_ids, slopes_b, n_req, n_pages)
        o_c = _sb_decode(pad_q(q_sb), pad_new(kv_sb), cache_sb_kv[l].reshape(n_phys, PAGE, 2 * D_SB),
                         page_ids, n_req, n_pages)
        unpad = lambda o: o[:, :n_new].reshape(n_req * n_new, -1)
        xs = _post(unpad(o_a), unpad(o_b), unpad(o_c), xs, g_mix[l], w_o[l], norm_ffn[l], w_up_b[l], w_down_b[l])
        new_win = kv_win.reshape(n_req, n_new, 2, KV_NSA, HEAD_DIM)
        all_win = jnp.concatenate([win_buf, new_win], axis=1)
        st_s[0].append(kv_moba.reshape(n_req, n_new, 2, H_MOBA, HEAD_DIM))
        st_s[1].append(kv_cmp.reshape(n_req, n_new, 2, KV_NSA, HEAD_DIM))
        st_s[2].append(kv_sel.reshape(n_req, n_new, 2, KV_NSA, HEAD_DIM))
        st_s[3].append(kv_sb.reshape(n_req, n_new, 2, H_SB, HEAD_DIM))
        st_s[4].append(all_win[:, all_win.shape[1] - min(WINDOW, all_win.shape[1]):])

    y_prompt = _final_norm(xp, norm_final).reshape(batch, seq, d_model)
    y_sample = _final_norm(xs, norm_final).reshape(n_req, n_new, d_model)
    new_state = []
    for i in range(5):
        new_state.append(jnp.stack(st_p[i]))
        new_state.append(jnp.stack(st_s[i]))
    return (y_prompt, y_sample, *new_state)
```

```python
import functools

import numpy as np
import jax
import jax.numpy as jnp
from jax import lax
from jax.experimental import pallas as pl
from jax.experimental.pallas import tpu as pltpu

F32 = jnp.float32
BF16 = jnp.bfloat16

HEAD_DIM = 64
H_MOBA = 6
H_NSA = 6
KV_NSA = 2
GROUP = H_NSA // KV_NSA
H_SB = 4
D_MOBA = H_MOBA * HEAD_DIM
D_NSA = H_NSA * HEAD_DIM
D_NSA_KV = KV_NSA * HEAD_DIM
D_SB = H_SB * HEAD_DIM
MOBA_BLOCK = 256
MOBA_TOPK = 3
CMP_LEN = 32
CMP_STRIDE = 16
CMP_HID = 128
SEL_BLOCK = 64
SEL_TOPK = 8
WINDOW = 512
N_GATES = 3 * H_NSA
PAGE = 128
EPS = 1e-6
NEG = -1e30
BIG = 1e30
SCALE = HEAD_DIM ** -0.5

LANES = 128
ROW_TILE = 512
NEW_ROWS = 16
Q_ROWS = 8
CMP_GROUP = 4
VMEM_LIMIT = 56 * 1024 * 1024

PAGES_PER_STEP = 8

_PROJ_TOK_S = (("q_moba", D_MOBA), ("kv_moba", 2 * D_MOBA), ("q_nsa", D_NSA), ("kv_cmp", 2 * D_NSA_KV),
               ("kv_sel", 2 * D_NSA_KV), ("kv_win", 2 * D_NSA_KV), ("q_sb", D_SB), ("kv_sb", 2 * D_SB),
               ("gates", LANES))
_PROJ_TOK_P = (("q_moba", D_MOBA), ("k_moba", D_MOBA), ("q_nsa", D_NSA), ("k_sel", D_NSA_KV), ("k_win", D_NSA_KV),
               ("q_sb", D_SB), ("k_sb", D_SB), ("gates", LANES))
_PROJ_FEAT_P = (("kvt_moba", 2 * D_MOBA), ("kvt_cmp", 2 * D_NSA_KV), ("kvt_sel", 2 * D_NSA_KV),
                ("kvt_win", 2 * D_NSA_KV), ("kvt_sb", 2 * D_SB))
_NSA_HEAD_ORDER = (0, 3, 1, 4, 2, 5)


def _alibi_slopes():
    n = H_MOBA + H_NSA
    s = 2.0 ** (-8.0 * np.arange(1, n + 1) / n)
    return jnp.asarray(s[0::2], F32), jnp.asarray(s[1::2], F32)


def _nt_dot(a, b, precision=None):
    return lax.dot_general(a, b, (((1,), (1,)), ((), ())), preferred_element_type=F32, precision=precision)


def _softplus(z):
    return jnp.maximum(z, 0.0) + jnp.log(1.0 + jnp.exp(-jnp.abs(z)))


def _cparams(sem):
    return pltpu.CompilerParams(dimension_semantics=sem, vmem_limit_bytes=VMEM_LIMIT)


def _in_proj_kernel(x_ref, g_ref, w_ref, *refs, tok, feat):
    x = x_ref[...]
    h = x * lax.rsqrt(jnp.mean(x * x, axis=-1, keepdims=True) + EPS) * g_ref[...]
    hb = h.astype(BF16)
    out_refs = refs[1:] if feat else refs
    off = 0
    for ref, (_, width) in zip(out_refs, tok):
        ref[...] = jnp.dot(hb, w_ref[:, off:off + width], preferred_element_type=F32)
        off += width
    if feat:
        wt_ref = refs[0]
        off = 0
        for ref, (_, width) in zip(out_refs[len(tok):], feat):
            ref[...] = _nt_dot(wt_ref[off:off + width, :], hb)
            off += width


def _in_proj(x2d, g, w_tok, tok, w_feat=None, feat=(), seq=None):
    n, d = x2d.shape
    tm = min(ROW_TILE, n)
    in_specs = [pl.BlockSpec((tm, d), lambda i: (i, 0)),
                pl.BlockSpec((1, d), lambda i: (0, 0)),
                pl.BlockSpec(w_tok.shape, lambda i: (0, 0))]
    args = [x2d, g.reshape(1, d), w_tok]
    out_specs = [pl.BlockSpec((tm, wd), lambda i: (i, 0)) for _, wd in tok]
    out_shape = [jax.ShapeDtypeStruct((n, wd), F32) for _, wd in tok]
    if feat:
        tiles = seq // tm
        in_specs.append(pl.BlockSpec(w_feat.shape, lambda i: (0, 0)))
        args.append(w_feat)
        out_specs += [pl.BlockSpec((None, wd, tm), lambda i: (i // tiles, 0, i % tiles)) for _, wd in feat]
        out_shape += [jax.ShapeDtypeStruct((n // seq, wd, seq), F32) for _, wd in feat]
    return pl.pallas_call(
        functools.partial(_in_proj_kernel, tok=tok, feat=feat),
        grid=(n // tm,),
        in_specs=in_specs,
        out_specs=out_specs,
        out_shape=out_shape,
        compiler_params=_cparams(("parallel",)),
        name="in_proj",
    )(*args)


def _post_kernel(oa_ref, ob_ref, oc_ref, x_ref, gmix_ref, wout_ref, gffn_ref, wup_ref, wdown_ref, out_ref, *, ff_chunk):
    def gnorm(o):
        return o * lax.rsqrt(jnp.mean(o * o, axis=-1, keepdims=True) + EPS)

    mixed = jnp.concatenate([gnorm(oa_ref[...]), gnorm(ob_ref[...]), gnorm(oc_ref[...])], axis=-1) * gmix_ref[...]
    x1 = x_ref[...] + jnp.dot(mixed.astype(BF16), wout_ref[...], preferred_element_type=F32)
    h2 = (x1 * lax.rsqrt(jnp.mean(x1 * x1, axis=-1, keepdims=True) + EPS) * gffn_ref[...]).astype(BF16)
    acc = x1
    d_ff = wup_ref.shape[1]
    for c in range(d_ff // ff_chunk):
        hid = jnp.dot(h2, wup_ref[:, c * ff_chunk:(c + 1) * ff_chunk], preferred_element_type=F32)
        hid = jnp.square(jnp.maximum(hid, 0.0)).astype(BF16)
        acc = acc + jnp.dot(hid, wdown_ref[c * ff_chunk:(c + 1) * ff_chunk, :], preferred_element_type=F32)
    out_ref[...] = acc


def _post(oa, ob, oc, x2d, gmix, wout, gffn, wup, wdown):
    n, d = x2d.shape
    tm = min(ROW_TILE, n)
    dmix = wout.shape[0]
    dff = wup.shape[1]
    const = lambda shape: pl.BlockSpec(shape, lambda i: (0, 0), pipeline_mode=pl.Buffered(1))
    row = lambda wd: pl.BlockSpec((tm, wd), lambda i: (i, 0))
    return pl.pallas_call(
        functools.partial(_post_kernel, ff_chunk=1024),
        grid=(n // tm,),
        in_specs=[row(oa.shape[1]), row(ob.shape[1]), row(oc.shape[1]), row(d),
                  const((1, dmix)), const((dmix, d)), const((1, d)), const((d, dff)), const((dff, d))],
        out_specs=row(d),
        out_shape=jax.ShapeDtypeStruct((n, d), F32),
        compiler_params=_cparams(("parallel",)),
        name="post_mlp",
    )(oa, ob, oc, x2d, gmix.reshape(1, dmix), wout, gffn.reshape(1, d), wup, wdown)


def _final_norm_kernel(x_ref, g_ref, o_ref):
    x = x_ref[...]
    o_ref[...] = x * lax.rsqrt(jnp.mean(x * x, axis=-1, keepdims=True) + EPS) * g_ref[...]


def _final_norm(x2d, g):
    n, d = x2d.shape
    tm = min(ROW_TILE, n)
    return pl.pallas_call(
        _final_norm_kernel,
        grid=(n // tm,),
        in_specs=[pl.BlockSpec((tm, d), lambda i: (i, 0)), pl.BlockSpec((1, d), lambda i: (0, 0))],
        out_specs=pl.BlockSpec((tm, d), lambda i: (i, 0)),
        out_shape=jax.ShapeDtypeStruct((n, d), F32),
        compiler_params=_cparams(("parallel",)),
        name="final_norm",
    )(x2d, g.reshape(1, d))


def _compress_kernel(*refs, npp, prefetch):
    refs = refs[prefetch:]
    page_refs = refs[:npp]
    pos_ref, w1_ref, w2_ref, out_ref, x_scr = refs[npp:]
    s = pl.program_id(1)
    rows_per_page = PAGE // CMP_STRIDE
    ri = lax.broadcasted_iota(jnp.int32, (PAGE, PAGE), 0)
    ci = lax.broadcasted_iota(jnp.int32, (PAGE, PAGE), 1)
    perm = (ci == CMP_STRIDE * (ri % rows_per_page) + ri // rows_per_page).astype(BF16)
    for i in range(npp):
        base = pl.multiple_of((s * npp + i) * rows_per_page, rows_per_page)
        pg = page_refs[i][...]
        hi = pg.astype(BF16)
        lo = (pg - hi.astype(F32)).astype(BF16)
        xp = _nt_dot(perm, hi) + _nt_dot(perm, lo)
        for l in range(CMP_STRIDE):
            x_scr[l, pl.ds(base, rows_per_page), :] = xp[l * rows_per_page:(l + 1) * rows_per_page, :]

    @pl.when(s == pl.num_programs(1) - 1)
    def _():
        rows = x_scr.shape[1]
        acc_a = jnp.zeros((rows, w1_ref.shape[2]), F32)
        acc_b = jnp.zeros((rows, w1_ref.shape[2]), F32)
        for l in range(CMP_STRIDE):
            x = x_scr[l]
            acc_a = acc_a + jnp.dot((x + pos_ref[l]).astype(BF16), w1_ref[l], preferred_element_type=F32)
            acc_b = acc_b + jnp.dot((x + pos_ref[CMP_STRIDE + l]).astype(BF16), w1_ref[CMP_STRIDE + l],
                                    preferred_element_type=F32)
        hidden = acc_a + pltpu.roll(acc_b, rows - 1, 0)
        out_ref[...] = jnp.dot(jnp.maximum(hidden, 0.0).astype(BF16), w2_ref[...], preferred_element_type=F32)


def _compress(src, layer, page_ids, n_req, n_pages, pos, w1, w2):
    group = min(CMP_GROUP, n_req)
    npp = min(PAGES_PER_STEP, n_pages)
    rows_req = n_pages * (PAGE // CMP_STRIDE)
    width = 2 * D_NSA_KV
    per_group = group * n_pages
    if page_ids is None:
        page_spec = lambda i: pl.BlockSpec(
            (None, width, PAGE),
            lambda g, s: (g * group + (s * npp + i) // n_pages, 0, (s * npp + i) % n_pages))
        const = lambda shape: pl.BlockSpec(shape, lambda g, s: (0,) * len(shape))
        out_spec = pl.BlockSpec((group * rows_req, width), lambda g, s: (g, 0))
        prefetch, args = 0, []
    else:
        page_spec = lambda i: pl.BlockSpec(
            (None, None, width, PAGE), lambda g, s, pt: (layer, pt[g * per_group + s * npp + i], 0, 0))
        const = lambda shape: pl.BlockSpec(shape, lambda g, s, pt: (0,) * len(shape))
        out_spec = pl.BlockSpec((group * rows_req, width), lambda g, s, pt: (g, 0))
        prefetch, args = 1, [page_ids]
    grid_spec = pltpu.PrefetchScalarGridSpec(
        num_scalar_prefetch=prefetch,
        grid=(n_req // group, per_group // npp),
        in_specs=[page_spec(i) for i in range(npp)] + [const(pos.shape), const(w1.shape), const(w2.shape)],
        out_specs=out_spec,
        scratch_shapes=[pltpu.VMEM((CMP_STRIDE, group * rows_req, width), F32)],
    )
    return pl.pallas_call(
        functools.partial(_compress_kernel, npp=npp, prefetch=prefetch),
        grid_spec=grid_spec,
        out_shape=jax.ShapeDtypeStruct((n_req * rows_req, width), F32),
        compiler_params=_cparams(("parallel", "arbitrary")),
        name="nsa_compress",
    )(*args, *([src] * npp), pos, w1, w2)


def _moba_prompt_kernel(slopes_ref, q_ref, k_ref, vt_ref, o_ref, kmean_scr, vt_scr, bias_scr, *, n_blocks):
    pr = pl.program_id(1)
    t = pl.program_id(2)
    tq = MOBA_BLOCK

    @pl.when(t == 0)
    def _():
        for n in range(n_blocks):
            kmean_scr[n:n + 1, :] = jnp.mean(k_ref[n * tq:(n + 1) * tq, :], axis=0, keepdims=True)
            vt_scr[:, n * tq:(n + 1) * tq] = vt_ref[:, n * tq:(n + 1) * tq].astype(BF16)

    q = q_ref[...]
    lane = lax.broadcasted_iota(jnp.int32, (1, LANES), 1)
    krow = lax.broadcasted_iota(jnp.int32, (tq, tq), 0)
    qcol = lax.broadcasted_iota(jnp.int32, (tq, tq), 1)
    d0 = (qcol - krow).astype(F32)
    blk = lax.broadcasted_iota(jnp.int32, (n_blocks, tq), 0)
    past = blk < t
    t0 = pl.multiple_of(t * tq, tq)
    k_own = k_ref[pl.ds(t0, tq), :].astype(BF16)
    qbs, slopes, init = [], [], []
    for hh in range(2):
        head_lanes = (lane < HEAD_DIM) if hh == 0 else (lane >= HEAD_DIM)
        qf = jnp.where(head_lanes, q, 0.0)
        qb = (qf * SCALE).astype(BF16)
        slope = slopes_ref[2 * pr + hh]
        gate = jnp.where(past, _nt_dot(kmean_scr[...], qf, lax.Precision.HIGHEST), NEG)
        rank = jnp.zeros((n_blocks, tq), jnp.int32)
        for m in range(n_blocks):
            gm = gate[m:m + 1, :]
            rank = rank + ((gm > gate) | ((gm == gate) & (m < blk))).astype(jnp.int32)
        bias_scr[hh] = jnp.where(past & (rank < MOBA_TOPK), 0.0, NEG)

        s = _nt_dot(k_own, qb) - slope * d0
        s = jnp.where(krow <= qcol, s, NEG)
        m0 = jnp.max(s, axis=0, keepdims=True)
        p = jnp.exp(s - m0)
        l0 = jnp.sum(p, axis=0, keepdims=True)
        acc0 = jnp.dot(vt_scr[hh * HEAD_DIM:(hh + 1) * HEAD_DIM, pl.ds(t0, tq)], p.astype(BF16),
                       preferred_element_type=F32)
        qbs.append(qb)
        slopes.append(slope)
        init.append((m0, l0, acc0))

    def body(j, carry):
        j0 = pl.multiple_of(j * tq, tq)
        kj = k_ref[pl.ds(j0, tq), :].astype(BF16)
        dist = d0 + ((t - j) * tq).astype(F32)
        out = []
        for hh in range(2):
            m_i, l_i, acc = carry[hh]
            sj = _nt_dot(kj, qbs[hh]) - slopes[hh] * dist + bias_scr[hh, pl.ds(j, 1), :]
            m_new = jnp.maximum(m_i, jnp.max(sj, axis=0, keepdims=True))
            alpha = jnp.exp(m_i - m_new)
            pj = jnp.exp(sj - m_new)
            l_new = alpha * l_i + jnp.sum(pj, axis=0, keepdims=True)
            acc = alpha * acc + jnp.dot(vt_scr[hh * HEAD_DIM:(hh + 1) * HEAD_DIM, pl.ds(j0, tq)], pj.astype(BF16),
                                        preferred_element_type=F32)
            out.append((m_new, l_new, acc))
        return tuple(out)

    final = lax.fori_loop(0, t, body, tuple(init))
    o_ref[...] = jnp.concatenate([acc_f / l_f for _, l_f, acc_f in final], axis=0).T


def _moba_prompt(q, k, kvt, slopes, batch, seq):
    n_blocks = seq // MOBA_BLOCK
    pairs = D_MOBA // LANES
    return pl.pallas_call(
        functools.partial(_moba_prompt_kernel, n_blocks=n_blocks),
        grid=(batch, pairs, n_blocks),
        in_specs=[pl.BlockSpec(memory_space=pltpu.SMEM),
                  pl.BlockSpec((MOBA_BLOCK, LANES), lambda b, p, t: (b * n_blocks + t, p)),
                  pl.BlockSpec((seq, LANES), lambda b, p, t: (b, p)),
                  pl.BlockSpec((None, LANES, seq), lambda b, p, t: (b, pairs + p, 0))],
        out_specs=pl.BlockSpec((MOBA_BLOCK, LANES), lambda b, p, t: (b * n_blocks + t, p)),
        out_shape=jax.ShapeDtypeStruct((batch * seq, D_MOBA), F32),
        scratch_shapes=[pltpu.VMEM((n_blocks, LANES), F32),
                        pltpu.VMEM((LANES, seq), BF16),
                        pltpu.VMEM((2, n_blocks, MOBA_BLOCK), F32)],
        compiler_params=_cparams(("parallel", "parallel", "arbitrary")),
        name="moba_prompt",
    )(slopes, q, k, kvt)


def _cumsum_after(strict_upper, lgt):
    hi = lgt.astype(BF16)
    lo = (lgt - hi.astype(F32)).astype(BF16)
    return (jnp.dot(strict_upper, hi, preferred_element_type=F32)
            + jnp.dot(strict_upper, lo, preferred_element_type=F32))


def _sb_prompt_kernel(q_ref, k_ref, vt_ref, o_ref, vt_scr, *, n_tiles, tq):
    t = pl.program_id(2)

    @pl.when(t == 0)
    def _():
        for n in range(n_tiles):
            vt_scr[:, n * tq:(n + 1) * tq] = vt_ref[:, n * tq:(n + 1) * tq].astype(BF16)

    q = q_ref[...]
    lane = lax.broadcasted_iota(jnp.int32, (1, LANES), 1)
    krow = lax.broadcasted_iota(jnp.int32, (tq, tq), 0)
    qcol = lax.broadcasted_iota(jnp.int32, (tq, tq), 1)
    causal = krow < qcol
    upper = (qcol > krow).astype(BF16)
    t0 = pl.multiple_of(t * tq, tq)
    k_own = k_ref[pl.ds(t0, tq), :].astype(BF16)
    qbs, init = [], []
    for hh in range(2):
        head_lanes = (lane < HEAD_DIM) if hh == 0 else (lane >= HEAD_DIM)
        qb = (jnp.where(head_lanes, q, 0.0) * SCALE).astype(BF16)
        z = _nt_dot(k_own, qb)
        sp = _softplus(z)
        lgt = jnp.where(causal, -sp, 0.0)
        a = jnp.where(causal, jnp.exp(z - sp + _cumsum_after(upper, lgt)), 0.0)
        acc0 = jnp.dot(vt_scr[hh * HEAD_DIM:(hh + 1) * HEAD_DIM, pl.ds(t0, tq)], a.astype(BF16),
                       preferred_element_type=F32)
        qbs.append(qb)
        init.append((jnp.sum(lgt, axis=0, keepdims=True), acc0))

    def body(i, carry):
        j0 = pl.multiple_of((t - 1 - i) * tq, tq)
        kj = k_ref[pl.ds(j0, tq), :].astype(BF16)
        out = []
        for hh in range(2):
            c, acc = carry[hh]
            zj = _nt_dot(kj, qbs[hh])
            spj = _softplus(zj)
            aj = jnp.exp(zj - spj + (_cumsum_after(upper, -spj) + c))
            acc = acc + jnp.dot(vt_scr[hh * HEAD_DIM:(hh + 1) * HEAD_DIM, pl.ds(j0, tq)], aj.astype(BF16),
                                preferred_element_type=F32)
            out.append((c - jnp.sum(spj, axis=0, keepdims=True), acc))
        return tuple(out)

    final = lax.fori_loop(0, t, body, tuple(init))
    o_ref[...] = jnp.concatenate([acc_f for _, acc_f in final], axis=0).T


def _sb_prompt(q, k, kvt, batch, seq):
    tq = 256
    n_tiles = seq // tq
    pairs = D_SB // LANES
    return pl.pallas_call(
        functools.partial(_sb_prompt_kernel, n_tiles=n_tiles, tq=tq),
        grid=(batch, pairs, n_tiles),
        in_specs=[pl.BlockSpec((tq, LANES), lambda b, p, t: (b * n_tiles + t, p)),
                  pl.BlockSpec((seq, LANES), lambda b, p, t: (b, p)),
                  pl.BlockSpec((None, LANES, seq), lambda b, p, t: (b, pairs + p, 0))],
        out_specs=pl.BlockSpec((tq, LANES), lambda b, p, t: (b * n_tiles + t, p)),
        out_shape=jax.ShapeDtypeStruct((batch * seq, D_SB), F32),
        scratch_shapes=[pltpu.VMEM((LANES, seq), BF16)],
        compiler_params=_cparams(("parallel", "parallel", "arbitrary")),
        name="sb_prompt",
    )(q, k, kvt)


def _nsa_prompt_kernel(slopes_ref, q_ref, gates_ref, cmp_ref, ksel_ref, kwin_ref, vtsel_ref, vtwin_ref, ovl_ref, o_ref,
                       vcmp_scr, vsel_scr, vwin_scr, bias_scr, *, seq):
    t = pl.program_id(1)
    tq = LANES
    n_tiles = seq // tq
    n_rows_cmp = seq // CMP_STRIDE
    n_cmp = (seq - CMP_LEN) // CMP_STRIDE + 1
    n_sel_blocks = seq // SEL_BLOCK
    n_top = min(SEL_TOPK, n_sel_blocks)
    wide = GROUP * tq

    @pl.when(t == 0)
    def _():
        vcmp_scr[...] = cmp_ref[:, LANES:2 * LANES].T.astype(BF16)
        for n in range(n_tiles):
            vsel_scr[:, n * tq:(n + 1) * tq] = vtsel_ref[:, n * tq:(n + 1) * tq].astype(BF16)
            vwin_scr[:, n * tq:(n + 1) * tq] = vtwin_ref[:, n * tq:(n + 1) * tq].astype(BF16)

    lane = lax.broadcasted_iota(jnp.int32, (1, LANES), 1)
    qi = lax.broadcasted_iota(jnp.int32, (1, wide), 1) % tq
    qpos = t * tq + qi
    krow = lax.broadcasted_iota(jnp.int32, (tq, wide), 0)
    d0 = qi - krow
    q = q_ref[...]
    gates_t = jax.nn.sigmoid(gates_ref[...]).T
    t0 = pl.multiple_of(t * tq, tq)

    qbs, slopes, o_cmps = [], [], []
    for g in range(KV_NSA):
        glanes = (lane < HEAD_DIM) if g == 0 else (lane >= HEAD_DIM)
        qg = jnp.concatenate([jnp.where(glanes, q[:, r * LANES:(r + 1) * LANES], 0.0) for r in range(GROUP)], axis=0)
        qb = (qg * SCALE).astype(BF16)
        slope = jnp.concatenate([jnp.full((1, tq), slopes_ref[g * GROUP + r], F32) for r in range(GROUP)], axis=1)
        vrows = slice(g * HEAD_DIM, (g + 1) * HEAD_DIM)

        nrow = lax.broadcasted_iota(jnp.int32, (n_rows_cmp, wide), 0)
        dist_c = qpos - (nrow * CMP_STRIDE + (CMP_LEN - 1))
        vis_c = (dist_c >= 0) & (nrow < n_cmp)
        sc = _nt_dot(cmp_ref[:, 0:LANES].astype(BF16), qb) - slope * dist_c.astype(F32)
        sc = jnp.where(vis_c, sc, NEG)
        pc = jnp.where(vis_c, jnp.exp(sc - jnp.max(sc, axis=0, keepdims=True)), 0.0)
        lc = jnp.sum(pc, axis=0, keepdims=True)
        pc = pc / jnp.where(lc > 0.0, lc, 1.0)
        o_cmp = jnp.dot(vcmp_scr[vrows, :], pc.astype(BF16), preferred_element_type=F32)

        psum = pc[:, 0:tq]
        for r in range(1, GROUP):
            psum = psum + pc[:, r * tq:(r + 1) * tq]
        imp = jnp.dot(ovl_ref[...], psum, preferred_element_type=F32, precision=lax.Precision.HIGHEST)
        blk = lax.broadcasted_iota(jnp.int32, (n_sel_blocks, tq), 0)
        own = qpos[:, 0:tq] // SEL_BLOCK
        cand = blk <= own
        score = jnp.where((blk == own) | (blk == 0), BIG, jnp.where(cand, imp, -BIG))
        rank = jnp.zeros((n_sel_blocks, tq), jnp.int32)
        for m in range(n_sel_blocks):
            sm = score[m:m + 1, :]
            rank = rank + ((sm > score) | ((sm == score) & (m < blk))).astype(jnp.int32)
        bias_scr[g] = jnp.where(cand & (rank < n_top), 0.0, NEG)
        qbs.append(qb)
        slopes.append(slope)
        o_cmps.append(o_cmp)

    def flash_step(carry, sj, vt):
        m_i, l_i, acc = carry
        m_new = jnp.maximum(m_i, jnp.max(sj, axis=0, keepdims=True))
        alpha = jnp.exp(m_i - m_new)
        pj = jnp.exp(sj - m_new)
        l_new = alpha * l_i + jnp.sum(pj, axis=0, keepdims=True)
        return m_new, l_new, alpha * acc + jnp.dot(vt, pj.astype(BF16), preferred_element_type=F32)

    def sel_body(j, carry):
        j0 = pl.multiple_of(j * tq, tq)
        kj = ksel_ref[pl.ds(j0, tq), :].astype(BF16)
        dist = d0 + (t - j) * tq
        out = []
        for g in range(KV_NSA):
            b0 = bias_scr[g, pl.ds(2 * j, 1), :]
            b1 = bias_scr[g, pl.ds(2 * j + 1, 1), :]
            bias = jnp.where(krow[:, 0:tq] < SEL_BLOCK, b0, b1)
            bias = jnp.concatenate([bias] * GROUP, axis=1)
            sj = _nt_dot(kj, qbs[g]) - slopes[g] * dist.astype(F32) + bias
            sj = jnp.where(dist >= 0, sj, NEG)
            out.append(flash_step(carry[g], sj, vsel_scr[g * HEAD_DIM:(g + 1) * HEAD_DIM, pl.ds(j0, tq)]))
        return tuple(out)

    init = (jnp.full((1, wide), NEG, F32), jnp.zeros((1, wide), F32), jnp.zeros((HEAD_DIM, wide), F32))
    sel = lax.fori_loop(0, t + 1, sel_body, (init,) * KV_NSA)

    def win_tile(j, carry):
        j0 = pl.multiple_of(j * tq, tq)
        kj = kwin_ref[pl.ds(j0, tq), :].astype(BF16)
        dist = d0 + (t - j) * tq
        out = []
        for g in range(KV_NSA):
            sj = _nt_dot(kj, qbs[g]) - slopes[g] * dist.astype(F32)
            sj = jnp.where((dist >= 0) & (dist < WINDOW), sj, NEG)
            out.append(flash_step(carry[g], sj, vwin_scr[g * HEAD_DIM:(g + 1) * HEAD_DIM, pl.ds(j0, tq)]))
        return tuple(out)

    win = lax.fori_loop(0, jnp.minimum(t, WINDOW // tq), lambda i, c: win_tile(t - 1 - i, c),
                        win_tile(t, (init,) * KV_NSA))

    for r in range(GROUP):
        halves = []
        for g in range(KV_NSA):
            h = g * GROUP + r
            sl = slice(r * tq, (r + 1) * tq)
            o_sel = sel[g][2][:, sl] / sel[g][1][:, sl]
            o_win = win[g][2][:, sl] / win[g][1][:, sl]
            halves.append(gates_t[h:h + 1, :] * o_cmps[g][:, sl] + gates_t[H_NSA + h:H_NSA + h + 1, :] * o_sel
                          + gates_t[2 * H_NSA + h:2 * H_NSA + h + 1, :] * o_win)
        o_ref[:, r * LANES:(r + 1) * LANES] = jnp.concatenate(halves, axis=0).T


def _nsa_prompt(q, gates, kcv, k_sel, k_win, kvt_sel, kvt_win, overlap_t, slopes, batch, seq):
    tq = LANES
    n_tiles = seq // tq
    n_rows_cmp = seq // CMP_STRIDE
    return pl.pallas_call(
        functools.partial(_nsa_prompt_kernel, seq=seq),
        grid=(batch, n_tiles),
        in_specs=[pl.BlockSpec(memory_space=pltpu.SMEM),
                  pl.BlockSpec((tq, D_NSA), lambda b, t: (b * n_tiles + t, 0)),
                  pl.BlockSpec((tq, LANES), lambda b, t: (b * n_tiles + t, 0)),
                  pl.BlockSpec((n_rows_cmp, 2 * LANES), lambda b, t: (b, 0)),
                  pl.BlockSpec((seq, LANES), lambda b, t: (b, 0)),
                  pl.BlockSpec((seq, LANES), lambda b, t: (b, 0)),
                  pl.BlockSpec((None, LANES, seq), lambda b, t: (b, 1, 0)),
                  pl.BlockSpec((None, LANES, seq), lambda b, t: (b, 1, 0)),
                  pl.BlockSpec(overlap_t.shape, lambda b, t: (0, 0))],
        out_specs=pl.BlockSpec((tq, D_NSA), lambda b, t: (b * n_tiles + t, 0)),
        out_shape=jax.ShapeDtypeStruct((batch * seq, D_NSA), F32),
        scratch_shapes=[pltpu.VMEM((LANES, n_rows_cmp), BF16),
                        pltpu.VMEM((LANES, seq), BF16),
                        pltpu.VMEM((LANES, seq), BF16),
                        pltpu.VMEM((KV_NSA, seq // SEL_BLOCK, tq), F32)],
        compiler_params=_cparams(("parallel", "arbitrary")),
        name="nsa_prompt",
    )(slopes, q, gates, kcv, k_sel, k_win, kvt_sel, kvt_win, overlap_t)


def _head_rows(q8, n_heads, lane_of_head):
    lanes = lax.broadcasted_iota(jnp.int32, (1, q8.shape[1]), 1)
    rows = []
    for h in range(n_heads):
        lo = lane_of_head(h)
        rows.append(jnp.where((lanes >= lo) & (lanes < lo + HEAD_DIM), q8, 0.0))
    return jnp.concatenate(rows, axis=0)


def _row_const(values, n_rows):
    row = lax.broadcasted_iota(jnp.int32, (n_rows, 1), 0) // Q_ROWS
    out = jnp.zeros((n_rows, 1), F32)
    for h, v in enumerate(values):
        out = jnp.where(row == h, v, out)
    return out


def _fold_heads(o, n_heads, lane_of_head):
    lanes = lax.broadcasted_iota(jnp.int32, (1, o.shape[1]), 1)
    out = jnp.zeros((Q_ROWS, o.shape[1]), F32)
    for h in range(n_heads):
        lo = lane_of_head(h)
        out = out + jnp.where((lanes >= lo) & (lanes < lo + HEAD_DIM), o[h * Q_ROWS:(h + 1) * Q_ROWS, :], 0.0)
    return out


def _moba_decode_kernel(pt_ref, slopes_ref, q_ref, new_ref, *refs, n_pages, npp, past):
    del pt_ref
    page_refs = refs[:npp]
    o_ref, qb_scr, slope_scr, new_scr, m_scr, l_scr, acc_scr, ksum_scr = refs[npp:]
    s = pl.program_id(1)
    rows = H_MOBA * Q_ROWS
    pages_per_block = MOBA_BLOCK // PAGE
    n_blocks = n_pages // pages_per_block
    lane_of_head = lambda h: h * HEAD_DIM
    qoff = lax.broadcasted_iota(jnp.int32, (rows, 1), 0) % Q_ROWS
    key = lax.broadcasted_iota(jnp.int32, (rows, PAGE), 1)
    klane = lax.broadcasted_iota(jnp.int32, (D_MOBA, LANES), 1)

    @pl.when(s == 0)
    def _():
        qb_scr[...] = (_head_rows(q_ref[...], H_MOBA, lane_of_head) * SCALE).astype(BF16)
        slope_scr[...] = jnp.broadcast_to(_row_const([slopes_ref[h] for h in range(H_MOBA)], rows), (rows, PAGE))
        ksum_scr[...] = jnp.zeros(ksum_scr.shape, F32)
        new_scr[...] = jnp.zeros(new_scr.shape, F32)
        new_scr[0:NEW_ROWS, :] = new_ref[...]
        dist = qoff - key
        sc = _nt_dot(qb_scr[...], new_scr[:, 0:D_MOBA].astype(BF16)) - slope_scr[...] * dist.astype(F32)
        sc = jnp.where(dist >= 0, sc, NEG)
        m = jnp.max(sc, axis=1, keepdims=True)
        pe = jnp.exp(sc - m)
        acc_scr[n_pages] = jnp.dot(pe.astype(BF16), new_scr[:, D_MOBA:2 * D_MOBA].astype(BF16),
                                   preferred_element_type=F32)
        m_scr[...] = jnp.where(key == n_pages, m, NEG)
        l_scr[...] = jnp.where(key == n_pages, jnp.sum(pe, axis=1, keepdims=True), 0.0)

    m_all = m_scr[...]
    l_all = l_scr[...]
    ksum = jnp.zeros((D_MOBA, LANES), F32)
    for i in range(npp):
        slot = s * npp + i
        kt = page_refs[i][0:D_MOBA, :]
        vt = page_refs[i][D_MOBA:2 * D_MOBA, :]
        dist = (past + qoff) - (slot * PAGE + key)
        sc = jnp.dot(qb_scr[...], kt.astype(BF16), preferred_element_type=F32) - slope_scr[...] * dist.astype(F32)
        m = jnp.max(sc, axis=1, keepdims=True)
        pe = jnp.exp(sc - m)
        acc_scr[slot] = _nt_dot(pe.astype(BF16), vt.astype(BF16))
        m_all = jnp.where(key == slot, m, m_all)
        l_all = jnp.where(key == slot, jnp.sum(pe, axis=1, keepdims=True), l_all)
        ksum = ksum + jnp.where(klane == slot // pages_per_block, jnp.sum(kt, axis=1, keepdims=True), 0.0)
    m_scr[...] = m_all
    l_scr[...] = l_all
    ksum_scr[...] = ksum_scr[...] + ksum

    @pl.when(s == pl.num_programs(1) - 1)
    def _():
        qf = _head_rows(q_ref[...], H_MOBA, lane_of_head)
        gate = jnp.dot(qf, ksum_scr[...] * (1.0 / MOBA_BLOCK), preferred_element_type=F32,
                       precision=lax.Precision.HIGHEST)
        is_block = key < n_blocks
        gate = jnp.where(is_block, gate, NEG)
        rank = jnp.zeros((rows, LANES), jnp.int32)
        for m in range(n_blocks):
            gm = gate[:, m:m + 1]
            rank = rank + ((gm > gate) | ((gm == gate) & (m < key))).astype(jnp.int32)
        chosen = (is_block & (rank < MOBA_TOPK)).astype(F32)
        blk_i = lax.broadcasted_iota(jnp.int32, (LANES, LANES), 0)
        slot_i = lax.broadcasted_iota(jnp.int32, (LANES, LANES), 1)
        expand = ((slot_i // pages_per_block == blk_i) & (slot_i < n_pages)).astype(F32)
        use = (jnp.dot(chosen, expand, preferred_element_type=F32) > 0.5) | (key == n_pages)
        m_tot = jnp.max(jnp.where(use, m_all, NEG), axis=1, keepdims=True)
        w = jnp.where(use, jnp.exp(m_all - m_tot), 0.0)
        den = jnp.sum(w * l_all, axis=1, keepdims=True)
        num = jnp.zeros((rows, D_MOBA), F32)
        for slot in range(n_pages + 1):
            num = num + w[:, slot:slot + 1] * acc_scr[slot]
        o_ref[...] = _fold_heads(num / den, H_MOBA, lane_of_head)


def _page_specs(layer, rows, n_pages, npp, reverse=False):
    def spec(i):
        if reverse:
            return pl.BlockSpec((None, None, rows, PAGE),
                                lambda b, s, pt: (layer, pt[b * n_pages + (n_pages - 1 - (s * npp + i))], 0, 0))
        return pl.BlockSpec((None, None, rows, PAGE), lambda b, s, pt: (layer, pt[b * n_pages + s * npp + i], 0, 0))
    return [spec(i) for i in range(npp)]


def _moba_decode(q8, new16, pool, layer, page_ids, slopes, n_req, n_pages):
    rows = H_MOBA * Q_ROWS
    past = n_pages * PAGE
    npp = min(PAGES_PER_STEP, n_pages)
    assert n_pages + 1 <= LANES
    grid_spec = pltpu.PrefetchScalarGridSpec(
        num_scalar_prefetch=1,
        grid=(n_req, n_pages // npp),
        in_specs=[pl.BlockSpec(memory_space=pltpu.SMEM),
                  pl.BlockSpec((None, Q_ROWS, D_MOBA), lambda b, s, pt: (b, 0, 0)),
                  pl.BlockSpec((None, NEW_ROWS, 2 * D_MOBA), lambda b, s, pt: (b, 0, 0))]
                 + _page_specs(layer, 2 * D_MOBA, n_pages, npp),
        out_specs=pl.BlockSpec((None, Q_ROWS, D_MOBA), lambda b, s, pt: (b, 0, 0)),
        scratch_shapes=[pltpu.VMEM((rows, D_MOBA), BF16),
                        pltpu.VMEM((rows, PAGE), F32),
                        pltpu.VMEM((PAGE, 2 * D_MOBA), F32),
                        pltpu.VMEM((rows, LANES), F32),
                        pltpu.VMEM((rows, LANES), F32),
                        pltpu.VMEM((n_pages + 1, rows, D_MOBA), F32),
                        pltpu.VMEM((D_MOBA, LANES), F32)],
    )
    return pl.pallas_call(
        functools.partial(_moba_decode_kernel, n_pages=n_pages, npp=npp, past=past),
        grid_spec=grid_spec,
        out_shape=jax.ShapeDtypeStruct((n_req, Q_ROWS, D_MOBA), F32),
        compiler_params=_cparams(("parallel", "arbitrary")),
        name="moba_decode",
    )(page_ids, slopes, q8, new16, *([pool] * npp))


def _sb_decode_kernel(pt_ref, q_ref, new_ref, *refs, npp):
    del pt_ref
    page_refs = refs[:npp]
    o_ref, qb_scr, new_scr, c_scr, acc_scr = refs[npp:]
    p = pl.program_id(1)
    rows = H_SB * Q_ROWS
    lane_of_head = lambda h: h * HEAD_DIM
    qoff = lax.broadcasted_iota(jnp.int32, (rows, 1), 0) % Q_ROWS
    key = lax.broadcasted_iota(jnp.int32, (rows, PAGE), 1)
    kj = lax.broadcasted_iota(jnp.int32, (PAGE, PAGE), 0)
    ks = lax.broadcasted_iota(jnp.int32, (PAGE, PAGE), 1)
    after_mat = (kj > ks).astype(BF16)

    def page_update(z, c, acc, causal, pv):
        sp = _softplus(z)
        lgt = -sp if causal is None else jnp.where(causal, -sp, 0.0)
        hi = lgt.astype(BF16)
        lo = (lgt - hi.astype(F32)).astype(BF16)
        after = (jnp.dot(hi, after_mat, preferred_element_type=F32)
                 + jnp.dot(lo, after_mat, preferred_element_type=F32)) + c
        a = jnp.exp(z - sp + after)
        if causal is not None:
            a = jnp.where(causal, a, 0.0)
        return c + jnp.sum(lgt, axis=1, keepdims=True), acc + pv(a.astype(BF16))

    @pl.when(p == 0)
    def _():
        qb_scr[...] = (_head_rows(q_ref[...], H_SB, lane_of_head) * SCALE).astype(BF16)
        new_scr[...] = jnp.zeros(new_scr.shape, F32)
        new_scr[0:NEW_ROWS, :] = new_ref[...]
        c0, acc0 = page_update(
            _nt_dot(qb_scr[...], new_scr[:, 0:D_SB].astype(BF16)),
            jnp.zeros((rows, PAGE), F32), jnp.zeros((rows, D_SB), F32), key < qoff,
            lambda a: jnp.dot(a, new_scr[:, D_SB:2 * D_SB].astype(BF16), preferred_element_type=F32))
        c_scr[...] = c0
        acc_scr[...] = acc0

    c = c_scr[...]
    acc = acc_scr[...]
    for i in range(npp):
        kt = page_refs[i][0:D_SB, :]
        vt = page_refs[i][D_SB:2 * D_SB, :]
        z = jnp.dot(qb_scr[...], kt.astype(BF16), preferred_element_type=F32)
        c, acc = page_update(z, c, acc, None, lambda a, vt=vt: _nt_dot(a, vt.astype(BF16)))
    c_scr[...] = c
    acc_scr[...] = acc

    @pl.when(p == pl.num_programs(1) - 1)
    def _():
        o_ref[...] = _fold_heads(acc, H_SB, lane_of_head)


def _sb_decode(q8, new16, pool, layer, page_ids, n_req, n_pages):
    rows = H_SB * Q_ROWS
    npp = min(PAGES_PER_STEP, n_pages)
    grid_spec = pltpu.PrefetchScalarGridSpec(
        num_scalar_prefetch=1,
        grid=(n_req, n_pages // npp),
        in_specs=[pl.BlockSpec((None, Q_ROWS, D_SB), lambda b, p, pt: (b, 0, 0)),
                  pl.BlockSpec((None, NEW_ROWS, 2 * D_SB), lambda b, p, pt: (b, 0, 0))]
                 + _page_specs(layer, 2 * D_SB, n_pages, npp, reverse=True),
        out_specs=pl.BlockSpec((None, Q_ROWS, D_SB), lambda b, p, pt: (b, 0, 0)),
        scratch_shapes=[pltpu.VMEM((rows, D_SB), BF16),
                        pltpu.VMEM((PAGE, 2 * D_SB), F32),
                        pltpu.VMEM((rows, PAGE), F32),
                        pltpu.VMEM((rows, D_SB), F32)],
    )
    return pl.pallas_call(
        functools.partial(_sb_decode_kernel, npp=npp),
        grid_spec=grid_spec,
        out_shape=jax.ShapeDtypeStruct((n_req, Q_ROWS, D_SB), F32),
        compiler_params=_cparams(("parallel", "arbitrary")),
        name="sb_decode",
    )(page_ids, q8, new16, *([pool] * npp))


def _nsa_decode_kernel(pt_ref, slopes_ref, q_ref, gates_ref, cmp_ref, selnew_ref, winbuf_ref, winnew_ref, ovl_ref,
                       *refs, npp, past):
    del pt_ref
    page_refs = refs[:npp]
    o_ref, qb_scr, slope_scr, new_scr, drop_scr, ocmp_scr, owin_scr, m_scr, l_scr, acc_scr = refs[npp:]
    p = pl.program_id(1)
    rows = H_NSA * Q_ROWS
    grows = KV_NSA * Q_ROWS
    n_rows_cmp = cmp_ref.shape[0]
    total = past + Q_ROWS // 2
    n_cmp = (total - CMP_LEN) // CMP_STRIDE + 1
    n_sel_blocks = -(-total // SEL_BLOCK)
    n_top = min(SEL_TOPK, n_sel_blocks)
    win_buf = winbuf_ref.shape[1]
    lane_of_head = lambda h: (h // GROUP) * HEAD_DIM
    qoff = lax.broadcasted_iota(jnp.int32, (rows, 1), 0) % Q_ROWS
    key = lax.broadcasted_iota(jnp.int32, (rows, PAGE), 1)

    @pl.when(p == 0)
    def _():
        q = q_ref[...]
        qh = jnp.concatenate([q[:, (h % GROUP) * LANES:(h % GROUP + 1) * LANES] for h in range(H_NSA)], axis=0)
        lanes = lax.broadcasted_iota(jnp.int32, (rows, LANES), 1)
        grp = lax.broadcasted_iota(jnp.int32, (rows, LANES), 0) // (GROUP * Q_ROWS)
        qh = jnp.where((lanes // HEAD_DIM) == grp, qh, 0.0)
        qb = (qh * SCALE).astype(BF16)
        qb_scr[...] = qb
        slope = _row_const([slopes_ref[h] for h in range(H_NSA)], rows)
        slope_scr[...] = jnp.broadcast_to(slope, (rows, PAGE))

        ncol = lax.broadcasted_iota(jnp.int32, (rows, n_rows_cmp), 1)
        dist_c = (past + qoff) - (ncol * CMP_STRIDE + (CMP_LEN - 1))
        vis_c = (dist_c >= 0) & (ncol < n_cmp)
        sc = _nt_dot(qb, cmp_ref[:, 0:LANES].astype(BF16)) - slope * dist_c.astype(F32)
        sc = jnp.where(vis_c, sc, NEG)
        pc = jnp.where(vis_c, jnp.exp(sc - jnp.max(sc, axis=1, keepdims=True)), 0.0)
        lc = jnp.sum(pc, axis=1, keepdims=True)
        pc = pc / jnp.where(lc > 0.0, lc, 1.0)
        ocmp_scr[...] = jnp.dot(pc.astype(BF16), cmp_ref[:, LANES:2 * LANES].astype(BF16), preferred_element_type=F32)

        psum = []
        for g in range(KV_NSA):
            acc = pc[g * GROUP * Q_ROWS:(g * GROUP + 1) * Q_ROWS, :]
            for r in range(1, GROUP):
                acc = acc + pc[(g * GROUP + r) * Q_ROWS:(g * GROUP + r + 1) * Q_ROWS, :]
            psum.append(acc)
        imp = jnp.dot(jnp.concatenate(psum, axis=0), ovl_ref[...], preferred_element_type=F32,
                      precision=lax.Precision.HIGHEST)
        blk = lax.broadcasted_iota(jnp.int32, (grows, LANES), 1)
        own = (past + lax.broadcasted_iota(jnp.int32, (grows, 1), 0) % Q_ROWS) // SEL_BLOCK
        cand = blk <= own
        score = jnp.where((blk == own) | (blk == 0), BIG, jnp.where(cand, imp, -BIG))
        rank = jnp.zeros((grows, LANES), jnp.int32)
        for m in range(n_sel_blocks):
            sm = score[:, m:m + 1]
            rank = rank + ((sm > score) | ((sm == score) & (m < blk))).astype(jnp.int32)
        drop_g = jnp.where(cand & (rank < n_top), 0.0, 1.0)
        drop_scr[...] = jnp.concatenate([drop_g[(h // GROUP) * Q_ROWS:(h // GROUP + 1) * Q_ROWS, :]
                                         for h in range(H_NSA)], axis=0).astype(BF16)

        new_scr[...] = jnp.zeros(new_scr.shape, F32)
        new_scr[0:NEW_ROWS, :] = winnew_ref[...]
        wkey = lax.broadcasted_iota(jnp.int32, (rows, win_buf), 1)
        dist_b = qoff + (win_buf - wkey)
        sb = (jnp.dot(qb, winbuf_ref[0:LANES, :].astype(BF16), preferred_element_type=F32)
              - slope * dist_b.astype(F32))
        sb = jnp.where((dist_b < WINDOW) & (past - win_buf + wkey >= 0), sb, NEG)
        dist_n = qoff - key
        sn = _nt_dot(qb, new_scr[:, 0:LANES].astype(BF16)) - slope * dist_n.astype(F32)
        sn = jnp.where(dist_n >= 0, sn, NEG)
        mw = jnp.maximum(jnp.max(sb, axis=1, keepdims=True), jnp.max(sn, axis=1, keepdims=True))
        pb = jnp.exp(sb - mw)
        pn = jnp.exp(sn - mw)
        lw = jnp.sum(pb, axis=1, keepdims=True) + jnp.sum(pn, axis=1, keepdims=True)
        ow = (_nt_dot(pb.astype(BF16), winbuf_ref[LANES:2 * LANES, :].astype(BF16))
              + jnp.dot(pn.astype(BF16), new_scr[:, LANES:2 * LANES].astype(BF16), preferred_element_type=F32))
        owin_scr[...] = ow / lw

        new_scr[0:NEW_ROWS, :] = selnew_ref[...]
        ss = _nt_dot(qb, new_scr[:, 0:LANES].astype(BF16)) - slope * dist_n.astype(F32)
        ss = jnp.where(dist_n >= 0, ss, NEG)
        m0 = jnp.max(ss, axis=1, keepdims=True)
        pe0 = jnp.exp(ss - m0)
        m_scr[...] = jnp.broadcast_to(m0, (rows, LANES))
        l_scr[...] = jnp.broadcast_to(jnp.sum(pe0, axis=1, keepdims=True), (rows, LANES))
        acc_scr[...] = jnp.dot(pe0.astype(BF16), new_scr[:, LANES:2 * LANES].astype(BF16), preferred_element_type=F32)

    step_tokens = npp * PAGE
    blk_i = lax.broadcasted_iota(jnp.int32, (LANES, step_tokens), 0)
    tok_i = lax.broadcasted_iota(jnp.int32, (LANES, step_tokens), 1)
    spread = (blk_i == p * (step_tokens // SEL_BLOCK) + tok_i // SEL_BLOCK).astype(BF16)
    drop = jnp.dot(drop_scr[...], spread, preferred_element_type=F32)
    scores = []
    for i in range(npp):
        kt = page_refs[i][0:LANES, :]
        dist = (past + qoff) - ((p * npp + i) * PAGE + key)
        sc = jnp.dot(qb_scr[...], kt.astype(BF16), preferred_element_type=F32) - slope_scr[...] * dist.astype(F32)
        scores.append(jnp.where(drop[:, i * PAGE:(i + 1) * PAGE] > 0.5, NEG, sc))
    m_old = m_scr[...][:, 0:1]
    m_new = m_old
    for sc in scores:
        m_new = jnp.maximum(m_new, jnp.max(sc, axis=1, keepdims=True))
    alpha = jnp.exp(m_old - m_new)
    l_new = alpha * l_scr[...][:, 0:1]
    acc = alpha * acc_scr[...]
    for i, sc in enumerate(scores):
        pe = jnp.exp(sc - m_new)
        l_new = l_new + jnp.sum(pe, axis=1, keepdims=True)
        acc = acc + _nt_dot(pe.astype(BF16), page_refs[i][LANES:2 * LANES, :].astype(BF16))
    m_scr[...] = jnp.broadcast_to(m_new, (rows, LANES))
    l_scr[...] = jnp.broadcast_to(l_new, (rows, LANES))
    acc_scr[...] = acc

    @pl.when(p == pl.num_programs(1) - 1)
    def _():
        o_sel = acc_scr[...] / l_scr[...][:, 0:1]
        gts = jax.nn.sigmoid(gates_ref[...])
        lane = lax.broadcasted_iota(jnp.int32, (1, LANES), 1)
        heads = []
        for h in range(H_NSA):
            sl = slice(h * Q_ROWS, (h + 1) * Q_ROWS)
            heads.append(gts[:, h:h + 1] * ocmp_scr[sl, :] + gts[:, H_NSA + h:H_NSA + h + 1] * o_sel[sl, :]
                         + gts[:, 2 * H_NSA + h:2 * H_NSA + h + 1] * owin_scr[sl, :])
        for r in range(GROUP):
            o_ref[:, r * LANES:(r + 1) * LANES] = jnp.where(lane < HEAD_DIM, heads[r], heads[GROUP + r])


def _nsa_decode(q8, gates8, kcv, selnew16, winbuf, winnew16, overlap, pool, layer, page_ids, slopes, n_req, n_pages):
    rows = H_NSA * Q_ROWS
    n_rows_cmp = n_pages * (PAGE // CMP_STRIDE)
    win_buf = winbuf.shape[3]
    npp = min(PAGES_PER_STEP, n_pages)
    assert -(-(n_pages * PAGE + Q_ROWS // 2) // SEL_BLOCK) <= LANES
    per_req = lambda shape: pl.BlockSpec((None,) + shape, lambda b, p, pt: (b, 0, 0))
    grid_spec = pltpu.PrefetchScalarGridSpec(
        num_scalar_prefetch=1,
        grid=(n_req, n_pages // npp),
        in_specs=[pl.BlockSpec(memory_space=pltpu.SMEM),
                  per_req((Q_ROWS, D_NSA)),
                  per_req((Q_ROWS, LANES)),
                  pl.BlockSpec((n_rows_cmp, 2 * LANES), lambda b, p, pt: (b, 0)),
                  per_req((NEW_ROWS, 2 * LANES)),
                  pl.BlockSpec((None, None, 2 * LANES, win_buf), lambda b, p, pt: (layer, b, 0, 0)),
                  per_req((NEW_ROWS, 2 * LANES)),
                  pl.BlockSpec(overlap.shape, lambda b, p, pt: (0, 0))]
                 + _page_specs(layer, 2 * LANES, n_pages, npp),
        out_specs=per_req((Q_ROWS, D_NSA)),
        scratch_shapes=[pltpu.VMEM((rows, LANES), BF16),
                        pltpu.VMEM((rows, PAGE), F32),
                        pltpu.VMEM((PAGE, 2 * LANES), F32),
                        pltpu.VMEM((rows, LANES), BF16),
                        pltpu.VMEM((rows, LANES), F32),
                        pltpu.VMEM((rows, LANES), F32),
                        pltpu.VMEM((rows, LANES), F32),
                        pltpu.VMEM((rows, LANES), F32),
                        pltpu.VMEM((rows, LANES), F32)],
    )
    return pl.pallas_call(
        functools.partial(_nsa_decode_kernel, npp=npp, past=n_pages * PAGE),
        grid_spec=grid_spec,
        out_shape=jax.ShapeDtypeStruct((n_req, Q_ROWS, D_NSA), F32),
        compiler_params=_cparams(("parallel", "arbitrary")),
        name="nsa_decode",
    )(page_ids, slopes, q8, gates8, kcv, selnew16, winbuf, winnew16, overlap, *([pool] * npp))


def _prep_layer_weights(w_in, cmp_pos, w_ck1, w_ck2, w_cv1, w_cv2, norm_mix, w_out):
    sizes = (D_MOBA, D_MOBA, D_MOBA, D_NSA, D_NSA_KV, D_NSA_KV, D_NSA_KV, D_NSA_KV, D_NSA_KV, D_NSA_KV,
             N_GATES, D_SB, D_SB, D_SB)
    offs = np.concatenate([[0], np.cumsum(sizes)])
    (qa, ka, va, qb, kc, vc, ks, vs, kw, vw, gt, qc, kcs, vcs) = [np.arange(offs[i], offs[i + 1]) for i in range(14)]
    qb_perm = np.concatenate([qb[h * HEAD_DIM:(h + 1) * HEAD_DIM] for h in _NSA_HEAD_ORDER])
    gate_pad = ((0, 0), (0, 0), (0, LANES - N_GATES))
    take = lambda cols: jnp.take(w_in, jnp.asarray(np.concatenate(cols)), axis=-1)
    w_tok_s = jnp.pad(take([qa, ka, va, qb_perm, kc, vc, ks, vs, kw, vw, qc, kcs, vcs, gt]), gate_pad).astype(BF16)
    w_tok_p = jnp.pad(take([qa, ka, qb_perm, ks, kw, qc, kcs, gt]), gate_pad).astype(BF16)
    w_feat_p = jnp.swapaxes(take([ka, va, kc, vc, ks, vs, kw, vw, kcs, vcs]), 1, 2).astype(BF16)
    w = (w_tok_s, w_tok_p, w_feat_p)

    nsa_perm = np.concatenate([D_MOBA + np.arange(h * HEAD_DIM, (h + 1) * HEAD_DIM) for h in _NSA_HEAD_ORDER])
    mix_perm = jnp.asarray(np.concatenate([np.arange(D_MOBA), nsa_perm, np.arange(D_MOBA + D_NSA, D_MOBA + D_NSA + D_SB)]))
    g_mix = jnp.take(norm_mix, mix_perm, axis=-1)
    wo = jnp.take(w_out, mix_perm, axis=-2).astype(BF16)

    depth = w_in.shape[0]
    k1 = w_ck1.reshape(depth, CMP_LEN, HEAD_DIM, CMP_HID)
    v1 = w_cv1.reshape(depth, CMP_LEN, HEAD_DIM, CMP_HID)
    z1 = jnp.zeros_like(k1)
    w1 = jnp.concatenate([jnp.concatenate([k1, z1, z1, z1], axis=-1), jnp.concatenate([z1, k1, z1, z1], axis=-1),
                          jnp.concatenate([z1, z1, v1, z1], axis=-1), jnp.concatenate([z1, z1, z1, v1], axis=-1)],
                         axis=-2).astype(BF16)
    z2 = jnp.zeros_like(w_ck2)
    w2 = jnp.concatenate([jnp.concatenate([w_ck2, z2, z2, z2], axis=-1), jnp.concatenate([z2, w_ck2, z2, z2], axis=-1),
                          jnp.concatenate([z2, z2, w_cv2, z2], axis=-1), jnp.concatenate([z2, z2, z2, w_cv2], axis=-1)],
                         axis=-2).astype(BF16)
    pos = jnp.concatenate([cmp_pos[:, 0], cmp_pos[:, 0], cmp_pos[:, 1], cmp_pos[:, 1]], axis=-1)[:, :, None, :]
    return w, g_mix, wo, w1, w2, pos


def _overlap(n_rows_cmp, n_cmp, n_blocks, n_cols):
    starts = np.arange(n_rows_cmp) * CMP_STRIDE
    sbs = np.arange(n_cols) * SEL_BLOCK
    ov = (starts[:, None] < sbs[None, :] + SEL_BLOCK) & (starts[:, None] + CMP_LEN > sbs[None, :])
    ov &= (np.arange(n_rows_cmp)[:, None] < n_cmp) & (np.arange(n_cols)[None, :] < n_blocks)
    return ov.astype(np.float32)


def kernel(x_prompt, x_sample, cache_moba_kv, cache_nsa_cmp_kv, cache_nsa_sel_kv, cache_sb_kv, state_nsa_win_kv, page_table, norm_attn, w_in, cmp_pos, w_cmp_k1, w_cmp_k2, w_cmp_v1, w_cmp_v2, norm_mix, w_out, norm_ffn, w_up, w_down, norm_final):
    batch, seq, d_model = x_prompt.shape
    n_req, n_new, _ = x_sample.shape
    depth = w_in.shape[0]
    n_pages = page_table.shape[1]
    past = n_pages * PAGE
    n_phys = cache_moba_kv.shape[1]
    assert n_new * 2 == Q_ROWS and seq % MOBA_BLOCK == 0 and past % MOBA_BLOCK == 0

    slopes_a, slopes_b = _alibi_slopes()
    w_proj, g_mix, w_o, w_c1, w_c2, pos_c = _prep_layer_weights(w_in, cmp_pos, w_cmp_k1, w_cmp_k2, w_cmp_v1, w_cmp_v2,
                                                                norm_mix, w_out)
    w_up_b = w_up.astype(BF16)
    w_down_b = w_down.astype(BF16)
    w_tok_s, w_tok_p, w_feat_p = w_proj
    page_ids = page_table.reshape(-1).astype(jnp.int32)

    def token_minor(c):
        lead = c.shape[:2]
        return jnp.transpose(c, (0, 1, 3, 4, 5, 2)).reshape(lead + (-1, c.shape[2]))

    def token_major(s, heads):
        lead, tokens = s.shape[:2], s.shape[3]
        return jnp.transpose(s.reshape(lead + (2, heads, HEAD_DIM, tokens)), (0, 1, 5, 2, 3, 4))

    pool_moba, pool_cmp, pool_sel, pool_sb = (token_minor(c) for c in
                                              (cache_moba_kv, cache_nsa_cmp_kv, cache_nsa_sel_kv, cache_sb_kv))
    win_state = token_minor(state_nsa_win_kv)

    n_rows_p = seq // CMP_STRIDE
    ovl_p = jnp.asarray(_overlap(n_rows_p, (seq - CMP_LEN) // CMP_STRIDE + 1, seq // SEL_BLOCK, seq // SEL_BLOCK).T)
    total = past + n_new
    ovl_s = jnp.asarray(_overlap(past // CMP_STRIDE, (total - CMP_LEN) // CMP_STRIDE + 1, -(-total // SEL_BLOCK), LANES))

    xp = x_prompt.reshape(batch * seq, d_model)
    xs = x_sample.reshape(n_req * n_new, d_model)
    pad_q = lambda a: jnp.pad(a.reshape(n_req, n_new, -1), ((0, 0), (0, Q_ROWS - n_new), (0, 0)))
    pad_new = lambda a: jnp.pad(a.reshape(n_req, n_new, -1), ((0, 0), (0, NEW_ROWS - n_new), (0, 0)))
    st_p = [[] for _ in range(5)]
    st_s = [[] for _ in range(5)]
    win_keep = min(WINDOW, seq)
    for l in range(depth):
        (q_moba, k_moba, q_nsa, k_sel, k_win, q_sb, k_sb, gates, kvt_moba, kvt_cmp, kvt_sel, kvt_win, kvt_sb) = _in_proj(
            xp, norm_attn[l], w_tok_p[l], _PROJ_TOK_P, w_feat_p[l], _PROJ_FEAT_P, seq)
        kcv = _compress(kvt_cmp, l, None, batch, seq // PAGE, pos_c[l], w_c1[l], w_c2[l])
        o_a = _moba_prompt(q_moba, k_moba, kvt_moba, slopes_a, batch, seq)
        o_b = _nsa_prompt(q_nsa, gates, kcv, k_sel, k_win, kvt_sel, kvt_win, ovl_p, slopes_b, batch, seq)
        o_c = _sb_prompt(q_sb, k_sb, kvt_sb, batch, seq)
        xp = _post(o_a, o_b, o_c, xp, g_mix[l], w_o[l], norm_ffn[l], w_up_b[l], w_down_b[l])
        for i, a in enumerate((kvt_moba, kvt_cmp, kvt_sel, kvt_sb, kvt_win[:, :, seq - win_keep:])):
            st_p[i].append(a)

        (q_moba, kv_moba, q_nsa, kv_cmp, kv_sel, kv_win, q_sb, kv_sb, gates) = _in_proj(
            xs, norm_attn[l], w_tok_s[l], _PROJ_TOK_S)
        kcv = _compress(pool_cmp, l, page_ids, n_req, n_pages, pos_c[l], w_c1[l], w_c2[l])
        o_a = _moba_decode(pad_q(q_moba), pad_new(kv_moba), pool_moba, l, page_ids, slopes_a, n_req, n_pages)
        o_b = _nsa_decode(pad_q(q_nsa), pad_q(gates), kcv, pad_new(kv_sel), win_state, pad_new(kv_win), ovl_s,
                          pool_sel, l, page_ids, slopes_b, n_req, n_pages)
        o_c = _sb_decode(pad_q(q_sb), pad_new(kv_sb), pool_sb, l, page_ids, n_req, n_pages)
        unpad = lambda o: o[:, :n_new].reshape(n_req * n_new, -1)
        xs = _post(unpad(o_a), unpad(o_b), unpad(o_c), xs, g_mix[l], w_o[l], norm_ffn[l], w_up_b[l], w_down_b[l])
        new_win_t = jnp.swapaxes(kv_win.reshape(n_req, n_new, 2 * D_NSA_KV), 1, 2)
        all_win = jnp.concatenate([win_state[l], new_win_t], axis=2)
        st_s[0].append(kv_moba.reshape(n_req, n_new, 2, H_MOBA, HEAD_DIM))
        st_s[1].append(kv_cmp.reshape(n_req, n_new, 2, KV_NSA, HEAD_DIM))
        st_s[2].append(kv_sel.reshape(n_req, n_new, 2, KV_NSA, HEAD_DIM))
        st_s[3].append(kv_sb.reshape(n_req, n_new, 2, H_SB, HEAD_DIM))
        st_s[4].append(all_win[:, :, all_win.shape[2] - min(WINDOW, all_win.shape[2]):])

    y_prompt = _final_norm(xp, norm_final).reshape(batch, seq, d_model)
    y_sample = _final_norm(xs, norm_final).reshape(n_req, n_new, d_model)
    heads = (H_MOBA, KV_NSA, KV_NSA, H_SB, KV_NSA)
    new_state = []
    for i in range(5):
        new_state.append(token_major(jnp.stack(st_p[i]), heads[i]))
        new_state.append(token_major(jnp.stack(st_s[i]), heads[i]) if i == 4 else jnp.stack(st_s[i]))
    return (y_prompt, y_sample, *new_state)
```

```python
import functools

import numpy as np
import jax
import jax.numpy as jnp
from jax import lax
from jax.experimental import pallas as pl
from jax.experimental.pallas import tpu as pltpu

F32 = jnp.float32
BF16 = jnp.bfloat16

HEAD_DIM = 64
H_MOBA = 6
H_NSA = 6
KV_NSA = 2
GROUP = H_NSA // KV_NSA
H_SB = 4
D_MOBA = H_MOBA * HEAD_DIM
D_NSA = H_NSA * HEAD_DIM
D_NSA_KV = KV_NSA * HEAD_DIM
D_SB = H_SB * HEAD_DIM
MOBA_BLOCK = 256
MOBA_TOPK = 3
CMP_LEN = 32
CMP_STRIDE = 16
CMP_HID = 128
SEL_BLOCK = 64
SEL_TOPK = 8
WINDOW = 512
N_GATES = 3 * H_NSA
PAGE = 128
EPS = 1e-6
NEG = -1e30
BIG = 1e30
SCALE = HEAD_DIM ** -0.5

LANES = 128
ROW_TILE = 512
NEW_ROWS = 16
Q_ROWS = 8
CMP_GROUP = 4
VMEM_LIMIT = 56 * 1024 * 1024

PAGES_PER_STEP = 8

_PROJ_TOK_S = (("q_moba", D_MOBA), ("kv_moba", 2 * D_MOBA), ("q_nsa", D_NSA), ("kv_cmp", 2 * D_NSA_KV),
               ("kv_sel", 2 * D_NSA_KV), ("kv_win", 2 * D_NSA_KV), ("q_sb", D_SB), ("kv_sb", 2 * D_SB),
               ("gates", LANES))
_PROJ_TOK_P = (("q_moba", D_MOBA), ("k_moba", D_MOBA), ("q_nsa", D_NSA), ("k_sel", D_NSA_KV), ("k_win", D_NSA_KV),
               ("q_sb", D_SB), ("k_sb", D_SB), ("gates", LANES))
_PROJ_FEAT_P = (("kvt_moba", 2 * D_MOBA), ("kvt_cmp", 2 * D_NSA_KV), ("kvt_sel", 2 * D_NSA_KV),
                ("kvt_win", 2 * D_NSA_KV), ("kvt_sb", 2 * D_SB))
_NSA_HEAD_ORDER = (0, 3, 1, 4, 2, 5)


def _alibi_slopes():
    n = H_MOBA + H_NSA
    s = 2.0 ** (-8.0 * np.arange(1, n + 1) / n)
    return jnp.asarray(s[0::2], F32), jnp.asarray(s[1::2], F32)


def _nt_dot(a, b, precision=None):
    return lax.dot_general(a, b, (((1,), (1,)), ((), ())), preferred_element_type=F32, precision=precision)


def _softplus(z):
    return jnp.maximum(z, 0.0) + jnp.log(1.0 + jnp.exp(-jnp.abs(z)))


def _merge_softmax(state, part):
    m_a, l_a, acc_a = state
    m_b, l_b, acc_b = part
    m = jnp.maximum(m_a, m_b)
    wa = jnp.exp(m_a - m)
    wb = jnp.exp(m_b - m)
    return m, wa * l_a + wb * l_b, wa * acc_a + wb * acc_b


def _cparams(sem):
    return pltpu.CompilerParams(dimension_semantics=sem, vmem_limit_bytes=VMEM_LIMIT)


def _in_proj_kernel(x_ref, g_ref, w_ref, *refs, tok, feat):
    x = x_ref[...]
    h = x * lax.rsqrt(jnp.mean(x * x, axis=-1, keepdims=True) + EPS) * g_ref[...]
    hb = h.astype(BF16)
    out_refs = refs[1:] if feat else refs
    off = 0
    for ref, (_, width) in zip(out_refs, tok):
        ref[...] = jnp.dot(hb, w_ref[:, off:off + width], preferred_element_type=F32)
        off += width
    if feat:
        wt_ref = refs[0]
        off = 0
        for ref, (_, width) in zip(out_refs[len(tok):], feat):
            ref[...] = _nt_dot(wt_ref[off:off + width, :], hb)
            off += width


def _in_proj(x2d, g, w_tok, tok, w_feat=None, feat=(), seq=None):
    n, d = x2d.shape
    tm = min(ROW_TILE, n)
    in_specs = [pl.BlockSpec((tm, d), lambda i: (i, 0)),
                pl.BlockSpec((1, d), lambda i: (0, 0)),
                pl.BlockSpec(w_tok.shape, lambda i: (0, 0))]
    args = [x2d, g.reshape(1, d), w_tok]
    out_specs = [pl.BlockSpec((tm, wd), lambda i: (i, 0)) for _, wd in tok]
    out_shape = [jax.ShapeDtypeStruct((n, wd), F32) for _, wd in tok]
    if feat:
        tiles = seq // tm
        in_specs.append(pl.BlockSpec(w_feat.shape, lambda i: (0, 0)))
        args.append(w_feat)
        out_specs += [pl.BlockSpec((None, wd, tm), lambda i: (i // tiles, 0, i % tiles)) for _, wd in feat]
        out_shape += [jax.ShapeDtypeStruct((n // seq, wd, seq), F32) for _, wd in feat]
    return pl.pallas_call(
        functools.partial(_in_proj_kernel, tok=tok, feat=feat),
        grid=(n // tm,),
        in_specs=in_specs,
        out_specs=out_specs,
        out_shape=out_shape,
        compiler_params=_cparams(("parallel",)),
        name="in_proj",
    )(*args)


def _post_kernel(oa_ref, ob_ref, oc_ref, x_ref, gmix_ref, wout_ref, gffn_ref, wup_ref, wdown_ref, out_ref, *, ff_chunk):
    def gnorm(o):
        return o * lax.rsqrt(jnp.mean(o * o, axis=-1, keepdims=True) + EPS)

    mixed = jnp.concatenate([gnorm(oa_ref[...]), gnorm(ob_ref[...]), gnorm(oc_ref[...])], axis=-1) * gmix_ref[...]
    x1 = x_ref[...] + jnp.dot(mixed.astype(BF16), wout_ref[...], preferred_element_type=F32)
    h2 = (x1 * lax.rsqrt(jnp.mean(x1 * x1, axis=-1, keepdims=True) + EPS) * gffn_ref[...]).astype(BF16)
    acc = x1
    d_ff = wup_ref.shape[1]
    for c in range(d_ff // ff_chunk):
        hid = jnp.dot(h2, wup_ref[:, c * ff_chunk:(c + 1) * ff_chunk], preferred_element_type=F32)
        hid = jnp.square(jnp.maximum(hid, 0.0)).astype(BF16)
        acc = acc + jnp.dot(hid, wdown_ref[c * ff_chunk:(c + 1) * ff_chunk, :], preferred_element_type=F32)
    out_ref[...] = acc


def _post(oa, ob, oc, x2d, gmix, wout, gffn, wup, wdown):
    n, d = x2d.shape
    tm = min(ROW_TILE, n)
    dmix = wout.shape[0]
    dff = wup.shape[1]
    const = lambda shape: pl.BlockSpec(shape, lambda i: (0, 0), pipeline_mode=pl.Buffered(1))
    row = lambda wd: pl.BlockSpec((tm, wd), lambda i: (i, 0))
    return pl.pallas_call(
        functools.partial(_post_kernel, ff_chunk=1024),
        grid=(n // tm,),
        in_specs=[row(oa.shape[1]), row(ob.shape[1]), row(oc.shape[1]), row(d),
                  const((1, dmix)), const((dmix, d)), const((1, d)), const((d, dff)), const((dff, d))],
        out_specs=row(d),
        out_shape=jax.ShapeDtypeStruct((n, d), F32),
        compiler_params=_cparams(("parallel",)),
        name="post_mlp",
    )(oa, ob, oc, x2d, gmix.reshape(1, dmix), wout, gffn.reshape(1, d), wup, wdown)


def _final_norm_kernel(x_ref, g_ref, o_ref):
    x = x_ref[...]
    o_ref[...] = x * lax.rsqrt(jnp.mean(x * x, axis=-1, keepdims=True) + EPS) * g_ref[...]


def _final_norm(x2d, g):
    n, d = x2d.shape
    tm = min(ROW_TILE, n)
    return pl.pallas_call(
        _final_norm_kernel,
        grid=(n // tm,),
        in_specs=[pl.BlockSpec((tm, d), lambda i: (i, 0)), pl.BlockSpec((1, d), lambda i: (0, 0))],
        out_specs=pl.BlockSpec((tm, d), lambda i: (i, 0)),
        out_shape=jax.ShapeDtypeStruct((n, d), F32),
        compiler_params=_cparams(("parallel",)),
        name="final_norm",
    )(x2d, g.reshape(1, d))


def _compress_kernel(*refs, npp, prefetch):
    refs = refs[prefetch:]
    page_refs = refs[:npp]
    pos_ref, w1_ref, w2_ref, out_ref, x_scr = refs[npp:]
    s = pl.program_id(1)
    rows_per_page = PAGE // CMP_STRIDE
    ri = lax.broadcasted_iota(jnp.int32, (PAGE, PAGE), 0)
    ci = lax.broadcasted_iota(jnp.int32, (PAGE, PAGE), 1)
    perm = (ci == CMP_STRIDE * (ri % rows_per_page) + ri // rows_per_page).astype(BF16)
    for i in range(npp):
        base = pl.multiple_of((s * npp + i) * rows_per_page, rows_per_page)
        pg = page_refs[i][...]
        hi = pg.astype(BF16)
        lo = (pg - hi.astype(F32)).astype(BF16)
        xp = _nt_dot(perm, hi) + _nt_dot(perm, lo)
        for l in range(CMP_STRIDE):
            x_scr[l, pl.ds(base, rows_per_page), :] = xp[l * rows_per_page:(l + 1) * rows_per_page, :]

    @pl.when(s == pl.num_programs(1) - 1)
    def _():
        rows = x_scr.shape[1]
        acc_a = jnp.zeros((rows, w1_ref.shape[2]), F32)
        acc_b = jnp.zeros((rows, w1_ref.shape[2]), F32)
        for l in range(CMP_STRIDE):
            x = x_scr[l]
            acc_a = acc_a + jnp.dot((x + pos_ref[l]).astype(BF16), w1_ref[l], preferred_element_type=F32)
            acc_b = acc_b + jnp.dot((x + pos_ref[CMP_STRIDE + l]).astype(BF16), w1_ref[CMP_STRIDE + l],
                                    preferred_element_type=F32)
        hidden = acc_a + pltpu.roll(acc_b, rows - 1, 0)
        out_ref[...] = jnp.dot(jnp.maximum(hidden, 0.0).astype(BF16), w2_ref[...], preferred_element_type=F32)


def _compress(src, layer, page_ids, n_req, n_pages, pos, w1, w2):
    group = min(CMP_GROUP, n_req)
    npp = min(PAGES_PER_STEP, n_pages)
    rows_req = n_pages * (PAGE // CMP_STRIDE)
    width = 2 * D_NSA_KV
    per_group = group * n_pages
    if page_ids is None:
        page_spec = lambda i: pl.BlockSpec(
            (None, width, PAGE),
            lambda g, s: (g * group + (s * npp + i) // n_pages, 0, (s * npp + i) % n_pages))
        const = lambda shape: pl.BlockSpec(shape, lambda g, s: (0,) * len(shape))
        out_spec = pl.BlockSpec((group * rows_req, width), lambda g, s: (g, 0))
        prefetch, args = 0, []
    else:
        page_spec = lambda i: pl.BlockSpec(
            (None, None, width, PAGE), lambda g, s, pt: (layer, pt[g * per_group + s * npp + i], 0, 0))
        const = lambda shape: pl.BlockSpec(shape, lambda g, s, pt: (0,) * len(shape))
        out_spec = pl.BlockSpec((group * rows_req, width), lambda g, s, pt: (g, 0))
        prefetch, args = 1, [page_ids]
    grid_spec = pltpu.PrefetchScalarGridSpec(
        num_scalar_prefetch=prefetch,
        grid=(n_req // group, per_group // npp),
        in_specs=[page_spec(i) for i in range(npp)] + [const(pos.shape), const(w1.shape), const(w2.shape)],
        out_specs=out_spec,
        scratch_shapes=[pltpu.VMEM((CMP_STRIDE, group * rows_req, width), F32)],
    )
    return pl.pallas_call(
        functools.partial(_compress_kernel, npp=npp, prefetch=prefetch),
        grid_spec=grid_spec,
        out_shape=jax.ShapeDtypeStruct((n_req * rows_req, width), F32),
        compiler_params=_cparams(("parallel", "arbitrary")),
        name="nsa_compress",
    )(*args, *([src] * npp), pos, w1, w2)


def _moba_prompt_kernel(slopes_ref, q_ref, k_ref, vt_ref, o_ref, kmean_scr, vt_scr, bias_scr, *, n_blocks):
    pr = pl.program_id(1)
    t = pl.program_id(2)
    tq = MOBA_BLOCK

    @pl.when(t == 0)
    def _():
        for n in range(n_blocks):
            kmean_scr[n:n + 1, :] = jnp.mean(k_ref[n * tq:(n + 1) * tq, :], axis=0, keepdims=True)
            vt_scr[:, n * tq:(n + 1) * tq] = vt_ref[:, n * tq:(n + 1) * tq].astype(BF16)

    q = q_ref[...]
    lane = lax.broadcasted_iota(jnp.int32, (1, LANES), 1)
    krow = lax.broadcasted_iota(jnp.int32, (tq, tq), 0)
    qcol = lax.broadcasted_iota(jnp.int32, (tq, tq), 1)
    d0 = (qcol - krow).astype(F32)
    blk = lax.broadcasted_iota(jnp.int32, (n_blocks, tq), 0)
    past = blk < t
    t0 = pl.multiple_of(t * tq, tq)
    k_own = k_ref[pl.ds(t0, tq), :].astype(BF16)
    qbs, slopes, init = [], [], []
    for hh in range(2):
        head_lanes = (lane < HEAD_DIM) if hh == 0 else (lane >= HEAD_DIM)
        qf = jnp.where(head_lanes, q, 0.0)
        qb = (qf * SCALE).astype(BF16)
        slope = slopes_ref[2 * pr + hh]
        gate = jnp.where(past, _nt_dot(kmean_scr[...], qf, lax.Precision.HIGHEST), NEG)
        rank = jnp.zeros((n_blocks, tq), jnp.int32)
        for m in range(n_blocks):
            gm = gate[m:m + 1, :]
            rank = rank + ((gm > gate) | ((gm == gate) & (m < blk))).astype(jnp.int32)
        bias_scr[hh] = jnp.where(past & (rank < MOBA_TOPK), 0.0, NEG)

        s = _nt_dot(k_own, qb) - slope * d0
        s = jnp.where(krow <= qcol, s, NEG)
        m0 = jnp.max(s, axis=0, keepdims=True)
        p = jnp.exp(s - m0)
        l0 = jnp.sum(p, axis=0, keepdims=True)
        acc0 = jnp.dot(vt_scr[hh * HEAD_DIM:(hh + 1) * HEAD_DIM, pl.ds(t0, tq)], p.astype(BF16),
                       preferred_element_type=F32)
        qbs.append(qb)
        slopes.append(slope)
        init.append((m0, l0, acc0))

    def body(i, carry):
        state = list(carry)
        chains = [(2 * i + u, hh) for u in range(2) for hh in range(2)]
        scores = []
        for j, hh in chains:
            j0 = pl.multiple_of(j * tq, tq)
            kj = k_ref[pl.ds(j0, tq), :].astype(BF16)
            scores.append(_nt_dot(kj, qbs[hh]))
        probs = []
        for (j, hh), s in zip(chains, scores):
            dist = d0 + ((t - j) * tq).astype(F32)
            sj = s - slopes[hh] * dist + bias_scr[hh, pl.ds(j, 1), :]
            m_j = jnp.max(sj, axis=0, keepdims=True)
            pj = jnp.exp(sj - m_j)
            probs.append((m_j, jnp.sum(pj, axis=0, keepdims=True), pj.astype(BF16)))
        for (j, hh), (m_j, l_j, pj) in zip(chains, probs):
            j0 = pl.multiple_of(j * tq, tq)
            pv = jnp.dot(vt_scr[hh * HEAD_DIM:(hh + 1) * HEAD_DIM, pl.ds(j0, tq)], pj, preferred_element_type=F32)
            state[hh] = _merge_softmax(state[hh], (m_j, l_j, pv))
        return tuple(state)

    final = lax.fori_loop(0, (t + 1) // 2, body, tuple(init))
    o_ref[...] = jnp.concatenate([acc_f / l_f for _, l_f, acc_f in final], axis=0).T


def _moba_prompt(q, k, kvt, slopes, batch, seq):
    n_blocks = seq // MOBA_BLOCK
    pairs = D_MOBA // LANES
    return pl.pallas_call(
        functools.partial(_moba_prompt_kernel, n_blocks=n_blocks),
        grid=(batch, pairs, n_blocks),
        in_specs=[pl.BlockSpec(memory_space=pltpu.SMEM),
                  pl.BlockSpec((MOBA_BLOCK, LANES), lambda b, p, t: (b * n_blocks + t, p)),
                  pl.BlockSpec((seq, LANES), lambda b, p, t: (b, p)),
                  pl.BlockSpec((None, LANES, seq), lambda b, p, t: (b, pairs + p, 0))],
        out_specs=pl.BlockSpec((MOBA_BLOCK, LANES), lambda b, p, t: (b * n_blocks + t, p)),
        out_shape=jax.ShapeDtypeStruct((batch * seq, D_MOBA), F32),
        scratch_shapes=[pltpu.VMEM((n_blocks, LANES), F32),
                        pltpu.VMEM((LANES, seq), BF16),
                        pltpu.VMEM((2, n_blocks, MOBA_BLOCK), F32)],
        compiler_params=_cparams(("parallel", "parallel", "arbitrary")),
        name="moba_prompt",
    )(slopes, q, k, kvt)


def _cumsum_after(strict_upper, lgt):
    hi = lgt.astype(BF16)
    lo = (lgt - hi.astype(F32)).astype(BF16)
    return (jnp.dot(strict_upper, hi, preferred_element_type=F32)
            + jnp.dot(strict_upper, lo, preferred_element_type=F32))


def _sb_prompt_kernel(q_ref, k_ref, vt_ref, o_ref, vt_scr, *, n_tiles, tq):
    t = pl.program_id(2)

    @pl.when(t == 0)
    def _():
        for n in range(n_tiles):
            vt_scr[:, n * tq:(n + 1) * tq] = vt_ref[:, n * tq:(n + 1) * tq].astype(BF16)

    q = q_ref[...]
    lane = lax.broadcasted_iota(jnp.int32, (1, LANES), 1)
    krow = lax.broadcasted_iota(jnp.int32, (tq, tq), 0)
    qcol = lax.broadcasted_iota(jnp.int32, (tq, tq), 1)
    causal = krow < qcol
    upper = (qcol > krow).astype(BF16)
    t0 = pl.multiple_of(t * tq, tq)
    k_own = k_ref[pl.ds(t0, tq), :].astype(BF16)
    qbs, init = [], []
    for hh in range(2):
        head_lanes = (lane < HEAD_DIM) if hh == 0 else (lane >= HEAD_DIM)
        qb = (jnp.where(head_lanes, q, 0.0) * SCALE).astype(BF16)
        z = _nt_dot(k_own, qb)
        sp = _softplus(z)
        lgt = jnp.where(causal, -sp, 0.0)
        a = jnp.where(causal, jnp.exp(z - sp + _cumsum_after(upper, lgt)), 0.0)
        acc0 = jnp.dot(vt_scr[hh * HEAD_DIM:(hh + 1) * HEAD_DIM, pl.ds(t0, tq)], a.astype(BF16),
                       preferred_element_type=F32)
        qbs.append(qb)
        init.append((jnp.sum(lgt, axis=0, keepdims=True), acc0))

    def body(i, carry):
        state = list(carry)
        chains = [(t - 1 - (2 * i + u), hh) for u in range(2) for hh in range(2)]
        zs = []
        for j, hh in chains:
            j0 = pl.multiple_of(jnp.maximum(j, 0) * tq, tq)
            zs.append(_nt_dot(k_ref[pl.ds(j0, tq), :].astype(BF16), qbs[hh]))
        sps = [_softplus(z) for z in zs]
        afters = [_cumsum_after(upper, -sp) for sp in sps]
        weights = []
        for (j, hh), z, sp, after in zip(chains, zs, sps, afters):
            c, acc = state[hh]
            weights.append(jnp.exp(z - sp + (after + c)).astype(BF16))
            state[hh] = (c - (j >= 0).astype(F32) * jnp.sum(sp, axis=0, keepdims=True), acc)
        for (j, hh), a in zip(chains, weights):
            j0 = pl.multiple_of(jnp.maximum(j, 0) * tq, tq)
            pv = jnp.dot(vt_scr[hh * HEAD_DIM:(hh + 1) * HEAD_DIM, pl.ds(j0, tq)], a, preferred_element_type=F32)
            c, acc = state[hh]
            state[hh] = (c, acc + (j >= 0).astype(F32) * pv)
        return tuple(state)

    final = lax.fori_loop(0, (t + 1) // 2, body, tuple(init))
    o_ref[...] = jnp.concatenate([acc_f for _, acc_f in final], axis=0).T


def _sb_prompt(q, k, kvt, batch, seq):
    tq = 256
    n_tiles = seq // tq
    pairs = D_SB // LANES
    return pl.pallas_call(
        functools.partial(_sb_prompt_kernel, n_tiles=n_tiles, tq=tq),
        grid=(batch, pairs, n_tiles),
        in_specs=[pl.BlockSpec((tq, LANES), lambda b, p, t: (b * n_tiles + t, p)),
                  pl.BlockSpec((seq, LANES), lambda b, p, t: (b, p)),
                  pl.BlockSpec((None, LANES, seq), lambda b, p, t: (b, pairs + p, 0))],
        out_specs=pl.BlockSpec((tq, LANES), lambda b, p, t: (b * n_tiles + t, p)),
        out_shape=jax.ShapeDtypeStruct((batch * seq, D_SB), F32),
        scratch_shapes=[pltpu.VMEM((LANES, seq), BF16)],
        compiler_params=_cparams(("parallel", "parallel", "arbitrary")),
        name="sb_prompt",
    )(q, k, kvt)


def _nsa_prompt_kernel(slopes_ref, q_ref, gates_ref, cmp_ref, ksel_ref, kwin_ref, vtsel_ref, vtwin_ref, ovl_ref, o_ref,
                       vcmp_scr, vsel_scr, vwin_scr, bias_scr, *, seq):
    t = pl.program_id(1)
    tq = LANES
    n_tiles = seq // tq
    n_rows_cmp = seq // CMP_STRIDE
    n_cmp = (seq - CMP_LEN) // CMP_STRIDE + 1
    n_sel_blocks = seq // SEL_BLOCK
    n_top = min(SEL_TOPK, n_sel_blocks)
    wide = GROUP * tq

    @pl.when(t == 0)
    def _():
        vcmp_scr[...] = cmp_ref[:, LANES:2 * LANES].T.astype(BF16)
        for n in range(n_tiles):
            vsel_scr[:, n * tq:(n + 1) * tq] = vtsel_ref[:, n * tq:(n + 1) * tq].astype(BF16)
            vwin_scr[:, n * tq:(n + 1) * tq] = vtwin_ref[:, n * tq:(n + 1) * tq].astype(BF16)

    lane = lax.broadcasted_iota(jnp.int32, (1, LANES), 1)
    qi = lax.broadcasted_iota(jnp.int32, (1, wide), 1) % tq
    qpos = t * tq + qi
    krow = lax.broadcasted_iota(jnp.int32, (tq, wide), 0)
    d0 = qi - krow
    q = q_ref[...]
    gates_t = jax.nn.sigmoid(gates_ref[...]).T
    t0 = pl.multiple_of(t * tq, tq)

    qbs, slopes, o_cmps = [], [], []
    for g in range(KV_NSA):
        glanes = (lane < HEAD_DIM) if g == 0 else (lane >= HEAD_DIM)
        qg = jnp.concatenate([jnp.where(glanes, q[:, r * LANES:(r + 1) * LANES], 0.0) for r in range(GROUP)], axis=0)
        qb = (qg * SCALE).astype(BF16)
        slope = jnp.concatenate([jnp.full((1, tq), slopes_ref[g * GROUP + r], F32) for r in range(GROUP)], axis=1)
        vrows = slice(g * HEAD_DIM, (g + 1) * HEAD_DIM)

        nrow = lax.broadcasted_iota(jnp.int32, (n_rows_cmp, wide), 0)
        dist_c = qpos - (nrow * CMP_STRIDE + (CMP_LEN - 1))
        vis_c = (dist_c >= 0) & (nrow < n_cmp)
        sc = _nt_dot(cmp_ref[:, 0:LANES].astype(BF16), qb) - slope * dist_c.astype(F32)
        sc = jnp.where(vis_c, sc, NEG)
        pc = jnp.where(vis_c, jnp.exp(sc - jnp.max(sc, axis=0, keepdims=True)), 0.0)
        lc = jnp.sum(pc, axis=0, keepdims=True)
        pc = pc / jnp.where(lc > 0.0, lc, 1.0)
        o_cmp = jnp.dot(vcmp_scr[vrows, :], pc.astype(BF16), preferred_element_type=F32)

        psum = pc[:, 0:tq]
        for r in range(1, GROUP):
            psum = psum + pc[:, r * tq:(r + 1) * tq]
        imp = jnp.dot(ovl_ref[...], psum, preferred_element_type=F32, precision=lax.Precision.HIGHEST)
        blk = lax.broadcasted_iota(jnp.int32, (n_sel_blocks, tq), 0)
        own = qpos[:, 0:tq] // SEL_BLOCK
        cand = blk <= own
        score = jnp.where((blk == own) | (blk == 0), BIG, jnp.where(cand, imp, -BIG))
        rank = jnp.zeros((n_sel_blocks, tq), jnp.int32)
        for m in range(n_sel_blocks):
            sm = score[m:m + 1, :]
            rank = rank + ((sm > score) | ((sm == score) & (m < blk))).astype(jnp.int32)
        bias_scr[g] = jnp.where(cand & (rank < n_top), 0.0, NEG)
        qbs.append(qb)
        slopes.append(slope)
        o_cmps.append(o_cmp)

    def run_chains(chains, k_ref_, vt_scr_, mask_fn, carry):
        state = list(carry)
        scores = []
        for j, g, _ in chains:
            j0 = pl.multiple_of(j * tq, tq)
            scores.append(_nt_dot(k_ref_[pl.ds(j0, tq), :].astype(BF16), qbs[g]))
        probs = []
        for (j, g, extra), s in zip(chains, scores):
            sj = mask_fn(j, g, s, extra)
            m_j = jnp.max(sj, axis=0, keepdims=True)
            pj = jnp.exp(sj - m_j)
            probs.append((m_j, jnp.sum(pj, axis=0, keepdims=True), pj.astype(BF16)))
        for (j, g, _), (m_j, l_j, pj) in zip(chains, probs):
            j0 = pl.multiple_of(j * tq, tq)
            pv = jnp.dot(vt_scr_[g * HEAD_DIM:(g + 1) * HEAD_DIM, pl.ds(j0, tq)], pj, preferred_element_type=F32)
            state[g] = _merge_softmax(state[g], (m_j, l_j, pv))
        return tuple(state)

    def sel_mask(j, g, s, _):
        b0 = bias_scr[g, pl.ds(2 * j, 1), :]
        b1 = bias_scr[g, pl.ds(2 * j + 1, 1), :]
        bias = jnp.where(krow[:, 0:tq] < SEL_BLOCK, b0, b1)
        bias = jnp.concatenate([bias] * GROUP, axis=1)
        dist = d0 + (t - j) * tq
        return jnp.where(dist >= 0, s - slopes[g] * dist.astype(F32) + bias, NEG)

    def sel_body(i, carry):
        chains = [(2 * i + u, g, None) for u in range(2) for g in range(KV_NSA)]
        return run_chains(chains, ksel_ref, vsel_scr, sel_mask, carry)

    init = (jnp.full((1, wide), NEG, F32), jnp.zeros((1, wide), F32), jnp.zeros((HEAD_DIM, wide), F32))
    sel = lax.fori_loop(0, (t + 2) // 2, sel_body, (init,) * KV_NSA)

    n_early = jnp.minimum(t, WINDOW // tq)

    def win_mask(j, g, s, k):
        dist = d0 + (t - j) * tq
        visible = (dist >= 0) & (dist < WINDOW) & (k <= n_early)
        return jnp.where(visible, s - slopes[g] * dist.astype(F32), NEG)

    def win_tiles(first, count, carry):
        chains = [(jnp.maximum(t - (first + u), 0), g, first + u) for u in range(count) for g in range(KV_NSA)]
        return run_chains(chains, kwin_ref, vwin_scr, win_mask, carry)

    win = lax.fori_loop(0, (n_early + 1) // 2, lambda i, c: win_tiles(1 + 2 * i, 2, c),
                        win_tiles(0, 1, (init,) * KV_NSA))

    for r in range(GROUP):
        halves = []
        for g in range(KV_NSA):
            h = g * GROUP + r
            sl = slice(r * tq, (r + 1) * tq)
            o_sel = sel[g][2][:, sl] / sel[g][1][:, sl]
            o_win = win[g][2][:, sl] / win[g][1][:, sl]
            halves.append(gates_t[h:h + 1, :] * o_cmps[g][:, sl] + gates_t[H_NSA + h:H_NSA + h + 1, :] * o_sel
                          + gates_t[2 * H_NSA + h:2 * H_NSA + h + 1, :] * o_win)
        o_ref[:, r * LANES:(r + 1) * LANES] = jnp.concatenate(halves, axis=0).T


def _nsa_prompt(q, gates, kcv, k_sel, k_win, kvt_sel, kvt_win, overlap_t, slopes, batch, seq):
    tq = LANES
    n_tiles = seq // tq
    n_rows_cmp = seq // CMP_STRIDE
    return pl.pallas_call(
        functools.partial(_nsa_prompt_kernel, seq=seq),
        grid=(batch, n_tiles),
        in_specs=[pl.BlockSpec(memory_space=pltpu.SMEM),
                  pl.BlockSpec((tq, D_NSA), lambda b, t: (b * n_tiles + t, 0)),
                  pl.BlockSpec((tq, LANES), lambda b, t: (b * n_tiles + t, 0)),
                  pl.BlockSpec((n_rows_cmp, 2 * LANES), lambda b, t: (b, 0)),
                  pl.BlockSpec((seq, LANES), lambda b, t: (b, 0)),
                  pl.BlockSpec((seq, LANES), lambda b, t: (b, 0)),
                  pl.BlockSpec((None, LANES, seq), lambda b, t: (b, 1, 0)),
                  pl.BlockSpec((None, LANES, seq), lambda b, t: (b, 1, 0)),
                  pl.BlockSpec(overlap_t.shape, lambda b, t: (0, 0))],
        out_specs=pl.BlockSpec((tq, D_NSA), lambda b, t: (b * n_tiles + t, 0)),
        out_shape=jax.ShapeDtypeStruct((batch * seq, D_NSA), F32),
        scratch_shapes=[pltpu.VMEM((LANES, n_rows_cmp), BF16),
                        pltpu.VMEM((LANES, seq), BF16),
                        pltpu.VMEM((LANES, seq), BF16),
                        pltpu.VMEM((KV_NSA, seq // SEL_BLOCK, tq), F32)],
        compiler_params=_cparams(("parallel", "arbitrary")),
        name="nsa_prompt",
    )(slopes, q, gates, kcv, k_sel, k_win, kvt_sel, kvt_win, overlap_t)


def _head_rows(q8, n_heads, lane_of_head):
    lanes = lax.broadcasted_iota(jnp.int32, (1, q8.shape[1]), 1)
    rows = []
    for h in range(n_heads):
        lo = lane_of_head(h)
        rows.append(jnp.where((lanes >= lo) & (lanes < lo + HEAD_DIM), q8, 0.0))
    return jnp.concatenate(rows, axis=0)


def _row_const(values, n_rows):
    row = lax.broadcasted_iota(jnp.int32, (n_rows, 1), 0) // Q_ROWS
    out = jnp.zeros((n_rows, 1), F32)
    for h, v in enumerate(values):
        out = jnp.where(row == h, v, out)
    return out


def _fold_heads(o, n_heads, lane_of_head):
    lanes = lax.broadcasted_iota(jnp.int32, (1, o.shape[1]), 1)
    out = jnp.zeros((Q_ROWS, o.shape[1]), F32)
    for h in range(n_heads):
        lo = lane_of_head(h)
        out = out + jnp.where((lanes >= lo) & (lanes < lo + HEAD_DIM), o[h * Q_ROWS:(h + 1) * Q_ROWS, :], 0.0)
    return out


def _moba_decode_kernel(pt_ref, slopes_ref, q_ref, new_ref, *refs, n_pages, npp, past):
    del pt_ref
    page_refs = refs[:npp]
    o_ref, qb_scr, slope_scr, new_scr, m_scr, l_scr, acc_scr, ksum_scr = refs[npp:]
    s = pl.program_id(1)
    rows = H_MOBA * Q_ROWS
    pages_per_block = MOBA_BLOCK // PAGE
    n_blocks = n_pages // pages_per_block
    lane_of_head = lambda h: h * HEAD_DIM
    qoff = lax.broadcasted_iota(jnp.int32, (rows, 1), 0) % Q_ROWS
    key = lax.broadcasted_iota(jnp.int32, (rows, PAGE), 1)
    klane = lax.broadcasted_iota(jnp.int32, (D_MOBA, LANES), 1)

    @pl.when(s == 0)
    def _():
        qb_scr[...] = (_head_rows(q_ref[...], H_MOBA, lane_of_head) * SCALE).astype(BF16)
        slope_scr[...] = jnp.broadcast_to(_row_const([slopes_ref[h] for h in range(H_MOBA)], rows), (rows, PAGE))
        ksum_scr[...] = jnp.zeros(ksum_scr.shape, F32)
        new_scr[...] = jnp.zeros(new_scr.shape, F32)
        new_scr[0:NEW_ROWS, :] = new_ref[...]
        dist = qoff - key
        sc = _nt_dot(qb_scr[...], new_scr[:, 0:D_MOBA].astype(BF16)) - slope_scr[...] * dist.astype(F32)
        sc = jnp.where(dist >= 0, sc, NEG)
        m = jnp.max(sc, axis=1, keepdims=True)
        pe = jnp.exp(sc - m)
        acc_scr[n_pages] = jnp.dot(pe.astype(BF16), new_scr[:, D_MOBA:2 * D_MOBA].astype(BF16),
                                   preferred_element_type=F32)
        m_scr[...] = jnp.where(key == n_pages, m, NEG)
        l_scr[...] = jnp.where(key == n_pages, jnp.sum(pe, axis=1, keepdims=True), 0.0)

    m_all = m_scr[...]
    l_all = l_scr[...]
    ksum = jnp.zeros((D_MOBA, LANES), F32)
    qb = qb_scr[...]
    scores = [jnp.dot(qb, page_refs[i][0:D_MOBA, :].astype(BF16), preferred_element_type=F32)
              for i in range(npp)]
    probs = []
    for i in range(npp):
        slot = s * npp + i
        dist = (past + qoff) - (slot * PAGE + key)
        sc = scores[i] - slope_scr[...] * dist.astype(F32)
        m = jnp.max(sc, axis=1, keepdims=True)
        pe = jnp.exp(sc - m)
        probs.append(pe.astype(BF16))
        m_all = jnp.where(key == slot, m, m_all)
        l_all = jnp.where(key == slot, jnp.sum(pe, axis=1, keepdims=True), l_all)
    for i in range(npp):
        acc_scr[s * npp + i] = _nt_dot(probs[i], page_refs[i][D_MOBA:2 * D_MOBA, :].astype(BF16))
    for i in range(npp):
        slot = s * npp + i
        kt = page_refs[i][0:D_MOBA, :]
        ksum = ksum + jnp.where(klane == slot // pages_per_block, jnp.sum(kt, axis=1, keepdims=True), 0.0)
    m_scr[...] = m_all
    l_scr[...] = l_all
    ksum_scr[...] = ksum_scr[...] + ksum

    @pl.when(s == pl.num_programs(1) - 1)
    def _():
        qf = _head_rows(q_ref[...], H_MOBA, lane_of_head)
        gate = jnp.dot(qf, ksum_scr[...] * (1.0 / MOBA_BLOCK), preferred_element_type=F32,
                       precision=lax.Precision.HIGHEST)
        is_block = key < n_blocks
        gate = jnp.where(is_block, gate, NEG)
        rank = jnp.zeros((rows, LANES), jnp.int32)
        for m in range(n_blocks):
            gm = gate[:, m:m + 1]
            rank = rank + ((gm > gate) | ((gm == gate) & (m < key))).astype(jnp.int32)
        chosen = (is_block & (rank < MOBA_TOPK)).astype(F32)
        blk_i = lax.broadcasted_iota(jnp.int32, (LANES, LANES), 0)
        slot_i = lax.broadcasted_iota(jnp.int32, (LANES, LANES), 1)
        expand = ((slot_i // pages_per_block == blk_i) & (slot_i < n_pages)).astype(F32)
        use = (jnp.dot(chosen, expand, preferred_element_type=F32) > 0.5) | (key == n_pages)
        m_tot = jnp.max(jnp.where(use, m_all, NEG), axis=1, keepdims=True)
        w = jnp.where(use, jnp.exp(m_all - m_tot), 0.0)
        den = jnp.sum(w * l_all, axis=1, keepdims=True)
        num = jnp.zeros((rows, D_MOBA), F32)
        for slot in range(n_pages + 1):
            num = num + w[:, slot:slot + 1] * acc_scr[slot]
        o_ref[...] = _fold_heads(num / den, H_MOBA, lane_of_head)


def _page_specs(layer, rows, n_pages, npp, reverse=False):
    def spec(i):
        if reverse:
            return pl.BlockSpec((None, None, rows, PAGE),
                                lambda b, s, pt: (layer, pt[b * n_pages + (n_pages - 1 - (s * npp + i))], 0, 0))
        return pl.BlockSpec((None, None, rows, PAGE), lambda b, s, pt: (layer, pt[b * n_pages + s * npp + i], 0, 0))
    return [spec(i) for i in range(npp)]


def _moba_decode(q8, new16, pool, layer, page_ids, slopes, n_req, n_pages):
    rows = H_MOBA * Q_ROWS
    past = n_pages * PAGE
    npp = min(PAGES_PER_STEP, n_pages)
    assert n_pages + 1 <= LANES
    grid_spec = pltpu.PrefetchScalarGridSpec(
        num_scalar_prefetch=1,
        grid=(n_req, n_pages // npp),
        in_specs=[pl.BlockSpec(memory_space=pltpu.SMEM),
                  pl.BlockSpec((None, Q_ROWS, D_MOBA), lambda b, s, pt: (b, 0, 0)),
                  pl.BlockSpec((None, NEW_ROWS, 2 * D_MOBA), lambda b, s, pt: (b, 0, 0))]
                 + _page_specs(layer, 2 * D_MOBA, n_pages, npp),
        out_specs=pl.BlockSpec((None, Q_ROWS, D_MOBA), lambda b, s, pt: (b, 0, 0)),
        scratch_shapes=[pltpu.VMEM((rows, D_MOBA), BF16),
                        pltpu.VMEM((rows, PAGE), F32),
                        pltpu.VMEM((PAGE, 2 * D_MOBA), F32),
                        pltpu.VMEM((rows, LANES), F32),
                        pltpu.VMEM((rows, LANES), F32),
                        pltpu.VMEM((n_pages + 1, rows, D_MOBA), F32),
                        pltpu.VMEM((D_MOBA, LANES), F32)],
    )
    return pl.pallas_call(
        functools.partial(_moba_decode_kernel, n_pages=n_pages, npp=npp, past=past),
        grid_spec=grid_spec,
        out_shape=jax.ShapeDtypeStruct((n_req, Q_ROWS, D_MOBA), F32),
        compiler_params=_cparams(("parallel", "arbitrary")),
        name="moba_decode",
    )(page_ids, slopes, q8, new16, *([pool] * npp))


def _sb_decode_kernel(pt_ref, q_ref, new_ref, *refs, npp):
    del pt_ref
    page_refs = refs[:npp]
    o_ref, qb_scr, new_scr, c_scr, acc_scr = refs[npp:]
    p = pl.program_id(1)
    rows = H_SB * Q_ROWS
    lane_of_head = lambda h: h * HEAD_DIM
    qoff = lax.broadcasted_iota(jnp.int32, (rows, 1), 0) % Q_ROWS
    key = lax.broadcasted_iota(jnp.int32, (rows, PAGE), 1)
    kj = lax.broadcasted_iota(jnp.int32, (PAGE, PAGE), 0)
    ks = lax.broadcasted_iota(jnp.int32, (PAGE, PAGE), 1)
    after_mat = (kj > ks).astype(BF16)

    def page_update(z, c, acc, causal, pv):
        sp = _softplus(z)
        lgt = -sp if causal is None else jnp.where(causal, -sp, 0.0)
        hi = lgt.astype(BF16)
        lo = (lgt - hi.astype(F32)).astype(BF16)
        after = (jnp.dot(hi, after_mat, preferred_element_type=F32)
                 + jnp.dot(lo, after_mat, preferred_element_type=F32)) + c
        a = jnp.exp(z - sp + after)
        if causal is not None:
            a = jnp.where(causal, a, 0.0)
        return c + jnp.sum(lgt, axis=1, keepdims=True), acc + pv(a.astype(BF16))

    @pl.when(p == 0)
    def _():
        qb_scr[...] = (_head_rows(q_ref[...], H_SB, lane_of_head) * SCALE).astype(BF16)
        new_scr[...] = jnp.zeros(new_scr.shape, F32)
        new_scr[0:NEW_ROWS, :] = new_ref[...]
        c0, acc0 = page_update(
            _nt_dot(qb_scr[...], new_scr[:, 0:D_SB].astype(BF16)),
            jnp.zeros((rows, PAGE), F32), jnp.zeros((rows, D_SB), F32), key < qoff,
            lambda a: jnp.dot(a, new_scr[:, D_SB:2 * D_SB].astype(BF16), preferred_element_type=F32))
        c_scr[...] = c0
        acc_scr[...] = acc0

    c = c_scr[...]
    acc = acc_scr[...]
    qb = qb_scr[...]
    zs = [jnp.dot(qb, page_refs[i][0:D_SB, :].astype(BF16), preferred_element_type=F32)
          for i in range(npp)]
    sps = [_softplus(z) for z in zs]
    afters = []
    for sp in sps:
        lgt = -sp
        hi = lgt.astype(BF16)
        lo = (lgt - hi.astype(F32)).astype(BF16)
        afters.append(jnp.dot(hi, after_mat, preferred_element_type=F32)
                      + jnp.dot(lo, after_mat, preferred_element_type=F32))
    weights = []
    for z, sp, after in zip(zs, sps, afters):
        weights.append(jnp.exp(z - sp + (after + c)).astype(BF16))
        c = c - jnp.sum(sp, axis=1, keepdims=True)
    for i in range(npp):
        acc = acc + _nt_dot(weights[i], page_refs[i][D_SB:2 * D_SB, :].astype(BF16))
    c_scr[...] = c
    acc_scr[...] = acc

    @pl.when(p == pl.num_programs(1) - 1)
    def _():
        o_ref[...] = _fold_heads(acc, H_SB, lane_of_head)


def _sb_decode(q8, new16, pool, layer, page_ids, n_req, n_pages):
    rows = H_SB * Q_ROWS
    npp = min(PAGES_PER_STEP, n_pages)
    grid_spec = pltpu.PrefetchScalarGridSpec(
        num_scalar_prefetch=1,
        grid=(n_req, n_pages // npp),
        in_specs=[pl.BlockSpec((None, Q_ROWS, D_SB), lambda b, p, pt: (b, 0, 0)),
                  pl.BlockSpec((None, NEW_ROWS, 2 * D_SB), lambda b, p, pt: (b, 0, 0))]
                 + _page_specs(layer, 2 * D_SB, n_pages, npp, reverse=True),
        out_specs=pl.BlockSpec((None, Q_ROWS, D_SB), lambda b, p, pt: (b, 0, 0)),
        scratch_shapes=[pltpu.VMEM((rows, D_SB), BF16),
                        pltpu.VMEM((PAGE, 2 * D_SB), F32),
                        pltpu.VMEM((rows, PAGE), F32),
                        pltpu.VMEM((rows, D_SB), F32)],
    )
    return pl.pallas_call(
        functools.partial(_sb_decode_kernel, npp=npp),
        grid_spec=grid_spec,
        out_shape=jax.ShapeDtypeStruct((n_req, Q_ROWS, D_SB), F32),
        compiler_params=_cparams(("parallel", "arbitrary")),
        name="sb_decode",
    )(page_ids, q8, new16, *([pool] * npp))


def _nsa_decode_kernel(pt_ref, slopes_ref, q_ref, gates_ref, cmp_ref, selnew_ref, winbuf_ref, winnew_ref, ovl_ref,
                       *refs, npp, past):
    del pt_ref
    page_refs = refs[:npp]
    o_ref, qb_scr, slope_scr, new_scr, drop_scr, ocmp_scr, owin_scr, m_scr, l_scr, acc_scr = refs[npp:]
    p = pl.program_id(1)
    rows = H_NSA * Q_ROWS
    grows = KV_NSA * Q_ROWS
    n_rows_cmp = cmp_ref.shape[0]
    total = past + Q_ROWS // 2
    n_cmp = (total - CMP_LEN) // CMP_STRIDE + 1
    n_sel_blocks = -(-total // SEL_BLOCK)
    n_top = min(SEL_TOPK, n_sel_blocks)
    win_buf = winbuf_ref.shape[1]
    lane_of_head = lambda h: (h // GROUP) * HEAD_DIM
    qoff = lax.broadcasted_iota(jnp.int32, (rows, 1), 0) % Q_ROWS
    key = lax.broadcasted_iota(jnp.int32, (rows, PAGE), 1)

    @pl.when(p == 0)
    def _():
        q = q_ref[...]
        qh = jnp.concatenate([q[:, (h % GROUP) * LANES:(h % GROUP + 1) * LANES] for h in range(H_NSA)], axis=0)
        lanes = lax.broadcasted_iota(jnp.int32, (rows, LANES), 1)
        grp = lax.broadcasted_iota(jnp.int32, (rows, LANES), 0) // (GROUP * Q_ROWS)
        qh = jnp.where((lanes // HEAD_DIM) == grp, qh, 0.0)
        qb = (qh * SCALE).astype(BF16)
        qb_scr[...] = qb
        slope = _row_const([slopes_ref[h] for h in range(H_NSA)], rows)
        slope_scr[...] = jnp.broadcast_to(slope, (rows, PAGE))

        ncol = lax.broadcasted_iota(jnp.int32, (rows, n_rows_cmp), 1)
        dist_c = (past + qoff) - (ncol * CMP_STRIDE + (CMP_LEN - 1))
        vis_c = (dist_c >= 0) & (ncol < n_cmp)
        sc = _nt_dot(qb, cmp_ref[:, 0:LANES].astype(BF16)) - slope * dist_c.astype(F32)
        sc = jnp.where(vis_c, sc, NEG)
        pc = jnp.where(vis_c, jnp.exp(sc - jnp.max(sc, axis=1, keepdims=True)), 0.0)
        lc = jnp.sum(pc, axis=1, keepdims=True)
        pc = pc / jnp.where(lc > 0.0, lc, 1.0)
        ocmp_scr[...] = jnp.dot(pc.astype(BF16), cmp_ref[:, LANES:2 * LANES].astype(BF16), preferred_element_type=F32)

        psum = []
        for g in range(KV_NSA):
            acc = pc[g * GROUP * Q_ROWS:(g * GROUP + 1) * Q_ROWS, :]
            for r in range(1, GROUP):
                acc = acc + pc[(g * GROUP + r) * Q_ROWS:(g * GROUP + r + 1) * Q_ROWS, :]
            psum.append(acc)
        imp = jnp.dot(jnp.concatenate(psum, axis=0), ovl_ref[...], preferred_element_type=F32,
                      precision=lax.Precision.HIGHEST)
        blk = lax.broadcasted_iota(jnp.int32, (grows, LANES), 1)
        own = (past + lax.broadcasted_iota(jnp.int32, (grows, 1), 0) % Q_ROWS) // SEL_BLOCK
        cand = blk <= own
        score = jnp.where((blk == own) | (blk == 0), BIG, jnp.where(cand, imp, -BIG))
        rank = jnp.zeros((grows, LANES), jnp.int32)
        for m in range(n_sel_blocks):
            sm = score[:, m:m + 1]
            rank = rank + ((sm > score) | ((sm == score) & (m < blk))).astype(jnp.int32)
        drop_g = jnp.where(cand & (rank < n_top), 0.0, 1.0)
        drop_scr[...] = jnp.concatenate([drop_g[(h // GROUP) * Q_ROWS:(h // GROUP + 1) * Q_ROWS, :]
                                         for h in range(H_NSA)], axis=0).astype(BF16)

        new_scr[...] = jnp.zeros(new_scr.shape, F32)
        new_scr[0:NEW_ROWS, :] = winnew_ref[...]
        wkey = lax.broadcasted_iota(jnp.int32, (rows, win_buf), 1)
        dist_b = qoff + (win_buf - wkey)
        sb = (jnp.dot(qb, winbuf_ref[0:LANES, :].astype(BF16), preferred_element_type=F32)
              - slope * dist_b.astype(F32))
        sb = jnp.where((dist_b < WINDOW) & (past - win_buf + wkey >= 0), sb, NEG)
        dist_n = qoff - key
        sn = _nt_dot(qb, new_scr[:, 0:LANES].astype(BF16)) - slope * dist_n.astype(F32)
        sn = jnp.where(dist_n >= 0, sn, NEG)
        mw = jnp.maximum(jnp.max(sb, axis=1, keepdims=True), jnp.max(sn, axis=1, keepdims=True))
        pb = jnp.exp(sb - mw)
        pn = jnp.exp(sn - mw)
        lw = jnp.sum(pb, axis=1, keepdims=True) + jnp.sum(pn, axis=1, keepdims=True)
        ow = (_nt_dot(pb.astype(BF16), winbuf_ref[LANES:2 * LANES, :].astype(BF16))
              + jnp.dot(pn.astype(BF16), new_scr[:, LANES:2 * LANES].astype(BF16), preferred_element_type=F32))
        owin_scr[...] = ow / lw

        new_scr[0:NEW_ROWS, :] = selnew_ref[...]
        ss = _nt_dot(qb, new_scr[:, 0:LANES].astype(BF16)) - slope * dist_n.astype(F32)
        ss = jnp.where(dist_n >= 0, ss, NEG)
        m0 = jnp.max(ss, axis=1, keepdims=True)
        pe0 = jnp.exp(ss - m0)
        m_scr[...] = jnp.broadcast_to(m0, (rows, LANES))
        l_scr[...] = jnp.broadcast_to(jnp.sum(pe0, axis=1, keepdims=True), (rows, LANES))
        acc_scr[...] = jnp.dot(pe0.astype(BF16), new_scr[:, LANES:2 * LANES].astype(BF16), preferred_element_type=F32)

    step_tokens = npp * PAGE
    blk_i = lax.broadcasted_iota(jnp.int32, (LANES, step_tokens), 0)
    tok_i = lax.broadcasted_iota(jnp.int32, (LANES, step_tokens), 1)
    spread = (blk_i == p * (step_tokens // SEL_BLOCK) + tok_i // SEL_BLOCK).astype(BF16)
    drop = jnp.dot(drop_scr[...], spread, preferred_element_type=F32)
    scores = []
    for i in range(npp):
        kt = page_refs[i][0:LANES, :]
        dist = (past + qoff) - ((p * npp + i) * PAGE + key)
        sc = jnp.dot(qb_scr[...], kt.astype(BF16), preferred_element_type=F32) - slope_scr[...] * dist.astype(F32)
        scores.append(jnp.where(drop[:, i * PAGE:(i + 1) * PAGE] > 0.5, NEG, sc))
    m_old = m_scr[...][:, 0:1]
    m_new = m_old
    for sc in scores:
        m_new = jnp.maximum(m_new, jnp.max(sc, axis=1, keepdims=True))
    alpha = jnp.exp(m_old - m_new)
    l_new = alpha * l_scr[...][:, 0:1]
    acc = alpha * acc_scr[...]
    probs = []
    for sc in scores:
        pe = jnp.exp(sc - m_new)
        l_new = l_new + jnp.sum(pe, axis=1, keepdims=True)
        probs.append(pe.astype(BF16))
    for i in range(npp):
        acc = acc + _nt_dot(probs[i], page_refs[i][LANES:2 * LANES, :].astype(BF16))
    m_scr[...] = jnp.broadcast_to(m_new, (rows, LANES))
    l_scr[...] = jnp.broadcast_to(l_new, (rows, LANES))
    acc_scr[...] = acc

    @pl.when(p == pl.num_programs(1) - 1)
    def _():
        o_sel = acc_scr[...] / l_scr[...][:, 0:1]
        gts = jax.nn.sigmoid(gates_ref[...])
        lane = lax.broadcasted_iota(jnp.int32, (1, LANES), 1)
        heads = []
        for h in range(H_NSA):
            sl = slice(h * Q_ROWS, (h + 1) * Q_ROWS)
            heads.append(gts[:, h:h + 1] * ocmp_scr[sl, :] + gts[:, H_NSA + h:H_NSA + h + 1] * o_sel[sl, :]
                         + gts[:, 2 * H_NSA + h:2 * H_NSA + h + 1] * owin_scr[sl, :])
        for r in range(GROUP):
            o_ref[:, r * LANES:(r + 1) * LANES] = jnp.where(lane < HEAD_DIM, heads[r], heads[GROUP + r])


def _nsa_decode(q8, gates8, kcv, selnew16, winbuf, winnew16, overlap, pool, layer, page_ids, slopes, n_req, n_pages):
    rows = H_NSA * Q_ROWS
    n_rows_cmp = n_pages * (PAGE // CMP_STRIDE)
    win_buf = winbuf.shape[3]
    npp = min(PAGES_PER_STEP, n_pages)
    assert -(-(n_pages * PAGE + Q_ROWS // 2) // SEL_BLOCK) <= LANES
    per_req = lambda shape: pl.BlockSpec((None,) + shape, lambda b, p, pt: (b, 0, 0))
    grid_spec = pltpu.PrefetchScalarGridSpec(
        num_scalar_prefetch=1,
        grid=(n_req, n_pages // npp),
        in_specs=[pl.BlockSpec(memory_space=pltpu.SMEM),
                  per_req((Q_ROWS, D_NSA)),
                  per_req((Q_ROWS, LANES)),
                  pl.BlockSpec((n_rows_cmp, 2 * LANES), lambda b, p, pt: (b, 0)),
                  per_req((NEW_ROWS, 2 * LANES)),
                  pl.BlockSpec((None, None, 2 * LANES, win_buf), lambda b, p, pt: (layer, b, 0, 0)),
                  per_req((NEW_ROWS, 2 * LANES)),
                  pl.BlockSpec(overlap.shape, lambda b, p, pt: (0, 0))]
                 + _page_specs(layer, 2 * LANES, n_pages, npp),
        out_specs=per_req((Q_ROWS, D_NSA)),
        scratch_shapes=[pltpu.VMEM((rows, LANES), BF16),
                        pltpu.VMEM((rows, PAGE), F32),
                        pltpu.VMEM((PAGE, 2 * LANES), F32),
                        pltpu.VMEM((rows, LANES), BF16),
                        pltpu.VMEM((rows, LANES), F32),
                        pltpu.VMEM((rows, LANES), F32),
                        pltpu.VMEM((rows, LANES), F32),
                        pltpu.VMEM((rows, LANES), F32),
                        pltpu.VMEM((rows, LANES), F32)],
    )
    return pl.pallas_call(
        functools.partial(_nsa_decode_kernel, npp=npp, past=n_pages * PAGE),
        grid_spec=grid_spec,
        out_shape=jax.ShapeDtypeStruct((n_req, Q_ROWS, D_NSA), F32),
        compiler_params=_cparams(("parallel", "arbitrary")),
        name="nsa_decode",
    )(page_ids, slopes, q8, gates8, kcv, selnew16, winbuf, winnew16, overlap, *([pool] * npp))


def _prep_layer_weights(w_in, cmp_pos, w_ck1, w_ck2, w_cv1, w_cv2, norm_mix, w_out):
    sizes = (D_MOBA, D_MOBA, D_MOBA, D_NSA, D_NSA_KV, D_NSA_KV, D_NSA_KV, D_NSA_KV, D_NSA_KV, D_NSA_KV,
             N_GATES, D_SB, D_SB, D_SB)
    offs = np.concatenate([[0], np.cumsum(sizes)])
    (qa, ka, va, qb, kc, vc, ks, vs, kw, vw, gt, qc, kcs, vcs) = [np.arange(offs[i], offs[i + 1]) for i in range(14)]
    qb_perm = np.concatenate([qb[h * HEAD_DIM:(h + 1) * HEAD_DIM] for h in _NSA_HEAD_ORDER])
    gate_pad = ((0, 0), (0, 0), (0, LANES - N_GATES))
    take = lambda cols: jnp.take(w_in, jnp.asarray(np.concatenate(cols)), axis=-1)
    w_tok_s = jnp.pad(take([qa, ka, va, qb_perm, kc, vc, ks, vs, kw, vw, qc, kcs, vcs, gt]), gate_pad).astype(BF16)
    w_tok_p = jnp.pad(take([qa, ka, qb_perm, ks, kw, qc, kcs, gt]), gate_pad).astype(BF16)
    w_feat_p = jnp.swapaxes(take([ka, va, kc, vc, ks, vs, kw, vw, kcs, vcs]), 1, 2).astype(BF16)
    w = (w_tok_s, w_tok_p, w_feat_p)

    nsa_perm = np.concatenate([D_MOBA + np.arange(h * HEAD_DIM, (h + 1) * HEAD_DIM) for h in _NSA_HEAD_ORDER])
    mix_perm = jnp.asarray(np.concatenate([np.arange(D_MOBA), nsa_perm, np.arange(D_MOBA + D_NSA, D_MOBA + D_NSA + D_SB)]))
    g_mix = jnp.take(norm_mix, mix_perm, axis=-1)
    wo = jnp.take(w_out, mix_perm, axis=-2).astype(BF16)

    depth = w_in.shape[0]
    k1 = w_ck1.reshape(depth, CMP_LEN, HEAD_DIM, CMP_HID)
    v1 = w_cv1.reshape(depth, CMP_LEN, HEAD_DIM, CMP_HID)
    z1 = jnp.zeros_like(k1)
    w1 = jnp.concatenate([jnp.concatenate([k1, z1, z1, z1], axis=-1), jnp.concatenate([z1, k1, z1, z1], axis=-1),
                          jnp.concatenate([z1, z1, v1, z1], axis=-1), jnp.concatenate([z1, z1, z1, v1], axis=-1)],
                         axis=-2).astype(BF16)
    z2 = jnp.zeros_like(w_ck2)
    w2 = jnp.concatenate([jnp.concatenate([w_ck2, z2, z2, z2], axis=-1), jnp.concatenate([z2, w_ck2, z2, z2], axis=-1),
                          jnp.concatenate([z2, z2, w_cv2, z2], axis=-1), jnp.concatenate([z2, z2, z2, w_cv2], axis=-1)],
                         axis=-2).astype(BF16)
    pos = jnp.concatenate([cmp_pos[:, 0], cmp_pos[:, 0], cmp_pos[:, 1], cmp_pos[:, 1]], axis=-1)[:, :, None, :]
    return w, g_mix, wo, w1, w2, pos


def _overlap(n_rows_cmp, n_cmp, n_blocks, n_cols):
    starts = np.arange(n_rows_cmp) * CMP_STRIDE
    sbs = np.arange(n_cols) * SEL_BLOCK
    ov = (starts[:, None] < sbs[None, :] + SEL_BLOCK) & (starts[:, None] + CMP_LEN > sbs[None, :])
    ov &= (np.arange(n_rows_cmp)[:, None] < n_cmp) & (np.arange(n_cols)[None, :] < n_blocks)
    return ov.astype(np.float32)


def kernel(x_prompt, x_sample, cache_moba_kv, cache_nsa_cmp_kv, cache_nsa_sel_kv, cache_sb_kv, state_nsa_win_kv, page_table, norm_attn, w_in, cmp_pos, w_cmp_k1, w_cmp_k2, w_cmp_v1, w_cmp_v2, norm_mix, w_out, norm_ffn, w_up, w_down, norm_final):
    batch, seq, d_model = x_prompt.shape
    n_req, n_new, _ = x_sample.shape
    depth = w_in.shape[0]
    n_pages = page_table.shape[1]
    past = n_pages * PAGE
    n_phys = cache_moba_kv.shape[1]
    assert n_new * 2 == Q_ROWS and seq % MOBA_BLOCK == 0 and past % MOBA_BLOCK == 0

    slopes_a, slopes_b = _alibi_slopes()
    w_proj, g_mix, w_o, w_c1, w_c2, pos_c = _prep_layer_weights(w_in, cmp_pos, w_cmp_k1, w_cmp_k2, w_cmp_v1, w_cmp_v2,
                                                                norm_mix, w_out)
    w_up_b = w_up.astype(BF16)
    w_down_b = w_down.astype(BF16)
    w_tok_s, w_tok_p, w_feat_p = w_proj
    page_ids = page_table.reshape(-1).astype(jnp.int32)

    def token_minor(c):
        lead = c.shape[:2]
        return jnp.transpose(c, (0, 1, 3, 4, 5, 2)).reshape(lead + (-1, c.shape[2]))

    def token_major(s, heads):
        lead, tokens = s.shape[:2], s.shape[3]
        return jnp.transpose(s.reshape(lead + (2, heads, HEAD_DIM, tokens)), (0, 1, 5, 2, 3, 4))

    pool_moba, pool_cmp, pool_sel, pool_sb = (token_minor(c) for c in
                                              (cache_moba_kv, cache_nsa_cmp_kv, cache_nsa_sel_kv, cache_sb_kv))
    win_state = token_minor(state_nsa_win_kv)

    n_rows_p = seq // CMP_STRIDE
    ovl_p = jnp.asarray(_overlap(n_rows_p, (seq - CMP_LEN) // CMP_STRIDE + 1, seq // SEL_BLOCK, seq // SEL_BLOCK).T)
    total = past + n_new
    ovl_s = jnp.asarray(_overlap(past // CMP_STRIDE, (total - CMP_LEN) // CMP_STRIDE + 1, -(-total // SEL_BLOCK), LANES))

    xp = x_prompt.reshape(batch * seq, d_model)
    xs = x_sample.reshape(n_req * n_new, d_model)
    pad_q = lambda a: jnp.pad(a.reshape(n_req, n_new, -1), ((0, 0), (0, Q_ROWS - n_new), (0, 0)))
    pad_new = lambda a: jnp.pad(a.reshape(n_req, n_new, -1), ((0, 0), (0, NEW_ROWS - n_new), (0, 0)))
    st_p = [[] for _ in range(5)]
    st_s = [[] for _ in range(5)]
    win_keep = min(WINDOW, seq)
    for l in range(depth):
        (q_moba, k_moba, q_nsa, k_sel, k_win, q_sb, k_sb, gates, kvt_moba, kvt_cmp, kvt_sel, kvt_win, kvt_sb) = _in_proj(
            xp, norm_attn[l], w_tok_p[l], _PROJ_TOK_P, w_feat_p[l], _PROJ_FEAT_P, seq)
        kcv = _compress(kvt_cmp, l, None, batch, seq // PAGE, pos_c[l], w_c1[l], w_c2[l])
        o_a = _moba_prompt(q_moba, k_moba, kvt_moba, slopes_a, batch, seq)
        o_b = _nsa_prompt(q_nsa, gates, kcv, k_sel, k_win, kvt_sel, kvt_win, ovl_p, slopes_b, batch, seq)
        o_c = _sb_prompt(q_sb, k_sb, kvt_sb, batch, seq)
        xp = _post(o_a, o_b, o_c, xp, g_mix[l], w_o[l], norm_ffn[l], w_up_b[l], w_down_b[l])
        for i, a in enumerate((kvt_moba, kvt_cmp, kvt_sel, kvt_sb, kvt_win[:, :, seq - win_keep:])):
            st_p[i].append(a)

        (q_moba, kv_moba, q_nsa, kv_cmp, kv_sel, kv_win, q_sb, kv_sb, gates) = _in_proj(
            xs, norm_attn[l], w_tok_s[l], _PROJ_TOK_S)
        kcv = _compress(pool_cmp, l, page_ids, n_req, n_pages, pos_c[l], w_c1[l], w_c2[l])
        o_a = _moba_decode(pad_q(q_moba), pad_new(kv_moba), pool_moba, l, page_ids, slopes_a, n_req, n_pages)
        o_b = _nsa_decode(pad_q(q_nsa), pad_q(gates), kcv, pad_new(kv_sel), win_state, pad_new(kv_win), ovl_s,
                          pool_sel, l, page_ids, slopes_b, n_req, n_pages)
        o_c = _sb_decode(pad_q(q_sb), pad_new(kv_sb), pool_sb, l, page_ids, n_req, n_pages)
        unpad = lambda o: o[:, :n_new].reshape(n_req * n_new, -1)
        xs = _post(unpad(o_a), unpad(o_b), unpad(o_c), xs, g_mix[l], w_o[l], norm_ffn[l], w_up_b[l], w_down_b[l])
        new_win_t = jnp.swapaxes(kv_win.reshape(n_req, n_new, 2 * D_NSA_KV), 1, 2)
        all_win = jnp.concatenate([win_state[l], new_win_t], axis=2)
        st_s[0].append(kv_moba.reshape(n_req, n_new, 2, H_MOBA, HEAD_DIM))
        st_s[1].append(kv_cmp.reshape(n_req, n_new, 2, KV_NSA, HEAD_DIM))
        st_s[2].append(kv_sel.reshape(n_req, n_new, 2, KV_NSA, HEAD_DIM))
        st_s[3].append(kv_sb.reshape(n_req, n_new, 2, H_SB, HEAD_DIM))
        st_s[4].append(all_win[:, :, all_win.shape[2] - min(WINDOW, all_win.shape[2]):])

    y_prompt = _final_norm(xp, norm_final).reshape(batch, seq, d_model)
    y_sample = _final_norm(xs, norm_final).reshape(n_req, n_new, d_model)
    heads = (H_MOBA, KV_NSA, KV_NSA, H_SB, KV_NSA)
    new_state = []
    for i in range(5):
        new_state.append(token_major(jnp.stack(st_p[i]), heads[i]))
        new_state.append(token_major(jnp.stack(st_s[i]), heads[i]) if i == 4 else jnp.stack(st_s[i]))
    return (y_prompt, y_sample, *new_state)
```

```python
import functools

import numpy as np
import jax
import jax.numpy as jnp
from jax import lax
from jax.experimental import pallas as pl
from jax.experimental.pallas import tpu as pltpu

F32 = jnp.float32
BF16 = jnp.bfloat16

HEAD_DIM = 64
H_MOBA = 6
H_NSA = 6
KV_NSA = 2
GROUP = H_NSA // KV_NSA
H_SB = 4
D_MOBA = H_MOBA * HEAD_DIM
D_NSA = H_NSA * HEAD_DIM
D_NSA_KV = KV_NSA * HEAD_DIM
D_SB = H_SB * HEAD_DIM
MOBA_BLOCK = 256
MOBA_TOPK = 3
CMP_LEN = 32
CMP_STRIDE = 16
CMP_HID = 128
SEL_BLOCK = 64
SEL_TOPK = 8
WINDOW = 512
N_GATES = 3 * H_NSA
PAGE = 128
EPS = 1e-6
NEG = -1e30
BIG = 1e30
SCALE = HEAD_DIM ** -0.5

LANES = 128
ROW_TILE = 512
NEW_ROWS = 16
Q_ROWS = 8
CMP_GROUP = 4
VMEM_LIMIT = 56 * 1024 * 1024

PAGES_PER_STEP = 16

_PROJ_TOK_S = (("q_moba", D_MOBA), ("kv_moba", 2 * D_MOBA), ("q_nsa", D_NSA), ("kv_cmp", 2 * D_NSA_KV),
               ("kv_sel", 2 * D_NSA_KV), ("kv_win", 2 * D_NSA_KV), ("q_sb", D_SB), ("kv_sb", 2 * D_SB),
               ("gates", LANES))
_PROJ_TOK_P = (("q_moba", D_MOBA), ("k_moba", D_MOBA), ("q_nsa", D_NSA), ("k_sel", D_NSA_KV), ("k_win", D_NSA_KV),
               ("q_sb", D_SB), ("k_sb", D_SB), ("gates", LANES))
_PROJ_FEAT_P = (("kvt_moba", 2 * D_MOBA), ("kvt_cmp", 2 * D_NSA_KV), ("kvt_sel", 2 * D_NSA_KV),
                ("kvt_win", 2 * D_NSA_KV), ("kvt_sb", 2 * D_SB))
_NSA_HEAD_ORDER = (0, 3, 1, 4, 2, 5)


def _alibi_slopes():
    n = H_MOBA + H_NSA
    s = 2.0 ** (-8.0 * np.arange(1, n + 1) / n)
    return jnp.asarray(s[0::2], F32), jnp.asarray(s[1::2], F32)


def _nt_dot(a, b, precision=None):
    return lax.dot_general(a, b, (((1,), (1,)), ((), ())), preferred_element_type=F32, precision=precision)


def _softplus(z):
    return jnp.maximum(z, 0.0) + jnp.log(1.0 + jnp.exp(-jnp.abs(z)))


def _merge_softmax(state, part):
    m_a, l_a, acc_a = state
    m_b, l_b, acc_b = part
    m = jnp.maximum(m_a, m_b)
    wa = jnp.exp(m_a - m)
    wb = jnp.exp(m_b - m)
    return m, wa * l_a + wb * l_b, wa * acc_a + wb * acc_b


def _cparams(sem):
    return pltpu.CompilerParams(dimension_semantics=sem, vmem_limit_bytes=VMEM_LIMIT)


def _in_proj_kernel(x_ref, g_ref, w_ref, *refs, tok, feat):
    x = x_ref[...]
    h = x * lax.rsqrt(jnp.mean(x * x, axis=-1, keepdims=True) + EPS) * g_ref[...]
    hb = h.astype(BF16)
    out_refs = refs[1:] if feat else refs
    off = 0
    for ref, (_, width) in zip(out_refs, tok):
        ref[...] = jnp.dot(hb, w_ref[:, off:off + width], preferred_element_type=F32)
        off += width
    if feat:
        wt_ref = refs[0]
        off = 0
        for ref, (_, width) in zip(out_refs[len(tok):], feat):
            ref[...] = _nt_dot(wt_ref[off:off + width, :], hb)
            off += width


def _in_proj(x2d, g, w_tok, tok, w_feat=None, feat=(), seq=None):
    n, d = x2d.shape
    tm = min(ROW_TILE, n)
    in_specs = [pl.BlockSpec((tm, d), lambda i: (i, 0)),
                pl.BlockSpec((1, d), lambda i: (0, 0)),
                pl.BlockSpec(w_tok.shape, lambda i: (0, 0))]
    args = [x2d, g.reshape(1, d), w_tok]
    out_specs = [pl.BlockSpec((tm, wd), lambda i: (i, 0)) for _, wd in tok]
    out_shape = [jax.ShapeDtypeStruct((n, wd), F32) for _, wd in tok]
    if feat:
        tiles = seq // tm
        in_specs.append(pl.BlockSpec(w_feat.shape, lambda i: (0, 0)))
        args.append(w_feat)
        out_specs += [pl.BlockSpec((None, wd, tm), lambda i: (i // tiles, 0, i % tiles)) for _, wd in feat]
        out_shape += [jax.ShapeDtypeStruct((n // seq, wd, seq), F32) for _, wd in feat]
    return pl.pallas_call(
        functools.partial(_in_proj_kernel, tok=tok, feat=feat),
        grid=(n // tm,),
        in_specs=in_specs,
        out_specs=out_specs,
        out_shape=out_shape,
        compiler_params=_cparams(("parallel",)),
        name="in_proj",
    )(*args)


def _post_kernel(oa_ref, ob_ref, oc_ref, x_ref, gmix_ref, wout_ref, gffn_ref, wup_ref, wdown_ref, out_ref, *, ff_chunk):
    def gnorm(o):
        return o * lax.rsqrt(jnp.mean(o * o, axis=-1, keepdims=True) + EPS)

    mixed = jnp.concatenate([gnorm(oa_ref[...]), gnorm(ob_ref[...]), gnorm(oc_ref[...])], axis=-1) * gmix_ref[...]
    x1 = x_ref[...] + jnp.dot(mixed.astype(BF16), wout_ref[...], preferred_element_type=F32)
    h2 = (x1 * lax.rsqrt(jnp.mean(x1 * x1, axis=-1, keepdims=True) + EPS) * gffn_ref[...]).astype(BF16)
    acc = x1
    d_ff = wup_ref.shape[1]
    for c in range(d_ff // ff_chunk):
        hid = jnp.dot(h2, wup_ref[:, c * ff_chunk:(c + 1) * ff_chunk], preferred_element_type=F32)
        hid = jnp.square(jnp.maximum(hid, 0.0)).astype(BF16)
        acc = acc + jnp.dot(hid, wdown_ref[c * ff_chunk:(c + 1) * ff_chunk, :], preferred_element_type=F32)
    out_ref[...] = acc


def _post(oa, ob, oc, x2d, gmix, wout, gffn, wup, wdown):
    n, d = x2d.shape
    tm = min(ROW_TILE, n)
    dmix = wout.shape[0]
    dff = wup.shape[1]
    const = lambda shape: pl.BlockSpec(shape, lambda i: (0, 0), pipeline_mode=pl.Buffered(1))
    row = lambda wd: pl.BlockSpec((tm, wd), lambda i: (i, 0))
    return pl.pallas_call(
        functools.partial(_post_kernel, ff_chunk=1024),
        grid=(n // tm,),
        in_specs=[row(oa.shape[1]), row(ob.shape[1]), row(oc.shape[1]), row(d),
                  const((1, dmix)), const((dmix, d)), const((1, d)), const((d, dff)), const((dff, d))],
        out_specs=row(d),
        out_shape=jax.ShapeDtypeStruct((n, d), F32),
        compiler_params=_cparams(("parallel",)),
        name="post_mlp",
    )(oa, ob, oc, x2d, gmix.reshape(1, dmix), wout, gffn.reshape(1, d), wup, wdown)


def _final_norm_kernel(x_ref, g_ref, o_ref):
    x = x_ref[...]
    o_ref[...] = x * lax.rsqrt(jnp.mean(x * x, axis=-1, keepdims=True) + EPS) * g_ref[...]


def _final_norm(x2d, g):
    n, d = x2d.shape
    tm = min(ROW_TILE, n)
    return pl.pallas_call(
        _final_norm_kernel,
        grid=(n // tm,),
        in_specs=[pl.BlockSpec((tm, d), lambda i: (i, 0)), pl.BlockSpec((1, d), lambda i: (0, 0))],
        out_specs=pl.BlockSpec((tm, d), lambda i: (i, 0)),
        out_shape=jax.ShapeDtypeStruct((n, d), F32),
        compiler_params=_cparams(("parallel",)),
        name="final_norm",
    )(x2d, g.reshape(1, d))


def _compress_kernel(*refs, npp, prefetch):
    refs = refs[prefetch:]
    page_refs = refs[:npp]
    pos_ref, w1_ref, w2_ref, out_ref, x_scr = refs[npp:]
    s = pl.program_id(1)
    rows_per_page = PAGE // CMP_STRIDE
    ri = lax.broadcasted_iota(jnp.int32, (PAGE, PAGE), 0)
    ci = lax.broadcasted_iota(jnp.int32, (PAGE, PAGE), 1)
    perm = (ci == CMP_STRIDE * (ri % rows_per_page) + ri // rows_per_page).astype(BF16)
    for i in range(npp):
        base = pl.multiple_of((s * npp + i) * rows_per_page, rows_per_page)
        pg = page_refs[i][...]
        hi = pg.astype(BF16)
        lo = (pg - hi.astype(F32)).astype(BF16)
        xp = _nt_dot(perm, hi) + _nt_dot(perm, lo)
        for l in range(CMP_STRIDE):
            x_scr[l, pl.ds(base, rows_per_page), :] = xp[l * rows_per_page:(l + 1) * rows_per_page, :]

    @pl.when(s == pl.num_programs(1) - 1)
    def _():
        rows = x_scr.shape[1]
        acc_a = jnp.zeros((rows, w1_ref.shape[2]), F32)
        acc_b = jnp.zeros((rows, w1_ref.shape[2]), F32)
        for l in range(CMP_STRIDE):
            x = x_scr[l]
            acc_a = acc_a + jnp.dot((x + pos_ref[l]).astype(BF16), w1_ref[l], preferred_element_type=F32)
            acc_b = acc_b + jnp.dot((x + pos_ref[CMP_STRIDE + l]).astype(BF16), w1_ref[CMP_STRIDE + l],
                                    preferred_element_type=F32)
        hidden = acc_a + pltpu.roll(acc_b, rows - 1, 0)
        out_ref[...] = jnp.dot(jnp.maximum(hidden, 0.0).astype(BF16), w2_ref[...], preferred_element_type=F32)


def _compress(src, layer, page_ids, n_req, n_pages, pos, w1, w2):
    group = min(CMP_GROUP, n_req)
    npp = min(PAGES_PER_STEP, n_pages)
    rows_req = n_pages * (PAGE // CMP_STRIDE)
    width = 2 * D_NSA_KV
    per_group = group * n_pages
    if page_ids is None:
        page_spec = lambda i: pl.BlockSpec(
            (None, width, PAGE),
            lambda g, s: (g * group + (s * npp + i) // n_pages, 0, (s * npp + i) % n_pages))
        const = lambda shape: pl.BlockSpec(shape, lambda g, s: (0,) * len(shape))
        out_spec = pl.BlockSpec((group * rows_req, width), lambda g, s: (g, 0))
        prefetch, args = 0, []
    else:
        page_spec = lambda i: pl.BlockSpec(
            (None, None, width, PAGE), lambda g, s, pt: (layer, pt[g * per_group + s * npp + i], 0, 0))
        const = lambda shape: pl.BlockSpec(shape, lambda g, s, pt: (0,) * len(shape))
        out_spec = pl.BlockSpec((group * rows_req, width), lambda g, s, pt: (g, 0))
        prefetch, args = 1, [page_ids]
    grid_spec = pltpu.PrefetchScalarGridSpec(
        num_scalar_prefetch=prefetch,
        grid=(n_req // group, per_group // npp),
        in_specs=[page_spec(i) for i in range(npp)] + [const(pos.shape), const(w1.shape), const(w2.shape)],
        out_specs=out_spec,
        scratch_shapes=[pltpu.VMEM((CMP_STRIDE, group * rows_req, width), F32)],
    )
    return pl.pallas_call(
        functools.partial(_compress_kernel, npp=npp, prefetch=prefetch),
        grid_spec=grid_spec,
        out_shape=jax.ShapeDtypeStruct((n_req * rows_req, width), F32),
        compiler_params=_cparams(("parallel", "arbitrary")),
        name="nsa_compress",
    )(*args, *([src] * npp), pos, w1, w2)


def _moba_prompt_kernel(slopes_ref, q_ref, k_ref, vt_ref, o_ref, kmean_scr, vt_scr, bias_scr, *, n_blocks):
    pr = pl.program_id(1)
    t = pl.program_id(2)
    tq = MOBA_BLOCK

    @pl.when(t == 0)
    def _():
        for n in range(n_blocks):
            kmean_scr[n:n + 1, :] = jnp.mean(k_ref[n * tq:(n + 1) * tq, :], axis=0, keepdims=True)
            vt_scr[:, n * tq:(n + 1) * tq] = vt_ref[:, n * tq:(n + 1) * tq].astype(BF16)

    q = q_ref[...]
    lane = lax.broadcasted_iota(jnp.int32, (1, LANES), 1)
    krow = lax.broadcasted_iota(jnp.int32, (tq, tq), 0)
    qcol = lax.broadcasted_iota(jnp.int32, (tq, tq), 1)
    d0 = (qcol - krow).astype(F32)
    blk = lax.broadcasted_iota(jnp.int32, (n_blocks, tq), 0)
    past = blk < t
    t0 = pl.multiple_of(t * tq, tq)
    k_own = k_ref[pl.ds(t0, tq), :].astype(BF16)
    qbs, slopes, init = [], [], []
    for hh in range(2):
        head_lanes = (lane < HEAD_DIM) if hh == 0 else (lane >= HEAD_DIM)
        qf = jnp.where(head_lanes, q, 0.0)
        qb = (qf * SCALE).astype(BF16)
        slope = slopes_ref[2 * pr + hh]
        gate = jnp.where(past, _nt_dot(kmean_scr[...], qf, lax.Precision.HIGHEST), NEG)
        rank = jnp.zeros((n_blocks, tq), jnp.int32)
        for m in range(n_blocks):
            gm = gate[m:m + 1, :]
            rank = rank + ((gm > gate) | ((gm == gate) & (m < blk))).astype(jnp.int32)
        bias_scr[hh] = jnp.where(past & (rank < MOBA_TOPK), 0.0, NEG)
        qbs.append(qb)
        slopes.append(slope)

    own_scores = [_nt_dot(k_own, qbs[hh]) for hh in range(2)]
    own_probs = []
    for hh in range(2):
        s = jnp.where(krow <= qcol, own_scores[hh] - slopes[hh] * d0, NEG)
        m0 = jnp.max(s, axis=0, keepdims=True)
        p = jnp.exp(s - m0)
        own_probs.append((m0, jnp.sum(p, axis=0, keepdims=True), p.astype(BF16)))
    for hh in range(2):
        m0, l0, p = own_probs[hh]
        init.append((m0, l0, jnp.dot(vt_scr[hh * HEAD_DIM:(hh + 1) * HEAD_DIM, pl.ds(t0, tq)], p,
                                     preferred_element_type=F32)))

    def body(i, carry):
        state = list(carry)
        chains = [(2 * i + u, hh) for u in range(2) for hh in range(2)]
        scores = []
        for j, hh in chains:
            j0 = pl.multiple_of(j * tq, tq)
            kj = k_ref[pl.ds(j0, tq), :].astype(BF16)
            scores.append(_nt_dot(kj, qbs[hh]))
        probs = []
        for (j, hh), s in zip(chains, scores):
            dist = d0 + ((t - j) * tq).astype(F32)
            sj = s - slopes[hh] * dist + bias_scr[hh, pl.ds(j, 1), :]
            m_j = jnp.max(sj, axis=0, keepdims=True)
            pj = jnp.exp(sj - m_j)
            probs.append((m_j, jnp.sum(pj, axis=0, keepdims=True), pj.astype(BF16)))
        for (j, hh), (m_j, l_j, pj) in zip(chains, probs):
            j0 = pl.multiple_of(j * tq, tq)
            pv = jnp.dot(vt_scr[hh * HEAD_DIM:(hh + 1) * HEAD_DIM, pl.ds(j0, tq)], pj, preferred_element_type=F32)
            state[hh] = _merge_softmax(state[hh], (m_j, l_j, pv))
        return tuple(state)

    final = lax.fori_loop(0, (t + 1) // 2, body, tuple(init))
    o_ref[...] = jnp.concatenate([acc_f / l_f for _, l_f, acc_f in final], axis=0).T


def _moba_prompt(q, k, kvt, slopes, batch, seq):
    n_blocks = seq // MOBA_BLOCK
    pairs = D_MOBA // LANES
    return pl.pallas_call(
        functools.partial(_moba_prompt_kernel, n_blocks=n_blocks),
        grid=(batch, pairs, n_blocks),
        in_specs=[pl.BlockSpec(memory_space=pltpu.SMEM),
                  pl.BlockSpec((MOBA_BLOCK, LANES), lambda b, p, t: (b * n_blocks + t, p)),
                  pl.BlockSpec((seq, LANES), lambda b, p, t: (b, p)),
                  pl.BlockSpec((None, LANES, seq), lambda b, p, t: (b, pairs + p, 0))],
        out_specs=pl.BlockSpec((MOBA_BLOCK, LANES), lambda b, p, t: (b * n_blocks + t, p)),
        out_shape=jax.ShapeDtypeStruct((batch * seq, D_MOBA), F32),
        scratch_shapes=[pltpu.VMEM((n_blocks, LANES), F32),
                        pltpu.VMEM((LANES, seq), BF16),
                        pltpu.VMEM((2, n_blocks, MOBA_BLOCK), F32)],
        compiler_params=_cparams(("parallel", "parallel", "arbitrary")),
        name="moba_prompt",
    )(slopes, q, k, kvt)


def _cumsum_after(strict_upper, lgt):
    hi = lgt.astype(BF16)
    lo = (lgt - hi.astype(F32)).astype(BF16)
    return (jnp.dot(strict_upper, hi, preferred_element_type=F32)
            + jnp.dot(strict_upper, lo, preferred_element_type=F32))


def _sb_prompt_kernel(q_ref, k_ref, vt_ref, o_ref, vt_scr, *, n_tiles, tq):
    t = pl.program_id(2)

    @pl.when(t == 0)
    def _():
        for n in range(n_tiles):
            vt_scr[:, n * tq:(n + 1) * tq] = vt_ref[:, n * tq:(n + 1) * tq].astype(BF16)

    q = q_ref[...]
    lane = lax.broadcasted_iota(jnp.int32, (1, LANES), 1)
    krow = lax.broadcasted_iota(jnp.int32, (tq, tq), 0)
    qcol = lax.broadcasted_iota(jnp.int32, (tq, tq), 1)
    causal = krow < qcol
    upper = (qcol > krow).astype(BF16)
    t0 = pl.multiple_of(t * tq, tq)
    k_own = k_ref[pl.ds(t0, tq), :].astype(BF16)
    qbs, init = [], []
    for hh in range(2):
        head_lanes = (lane < HEAD_DIM) if hh == 0 else (lane >= HEAD_DIM)
        qbs.append((jnp.where(head_lanes, q, 0.0) * SCALE).astype(BF16))
    own_z = [_nt_dot(k_own, qbs[hh]) for hh in range(2)]
    own_sp = [_softplus(z) for z in own_z]
    own_lgt = [jnp.where(causal, -sp, 0.0) for sp in own_sp]
    own_after = [_cumsum_after(upper, lgt) for lgt in own_lgt]
    own_a = [jnp.where(causal, jnp.exp(z - sp + after), 0.0).astype(BF16)
             for z, sp, after in zip(own_z, own_sp, own_after)]
    for hh in range(2):
        acc0 = jnp.dot(vt_scr[hh * HEAD_DIM:(hh + 1) * HEAD_DIM, pl.ds(t0, tq)], own_a[hh],
                       preferred_element_type=F32)
        init.append((jnp.sum(own_lgt[hh], axis=0, keepdims=True), acc0))

    def body(i, carry):
        state = list(carry)
        chains = [(t - 1 - (2 * i + u), hh) for u in range(2) for hh in range(2)]
        zs = []
        for j, hh in chains:
            j0 = pl.multiple_of(jnp.maximum(j, 0) * tq, tq)
            zs.append(_nt_dot(k_ref[pl.ds(j0, tq), :].astype(BF16), qbs[hh]))
        sps = [_softplus(z) for z in zs]
        afters = [_cumsum_after(upper, -sp) for sp in sps]
        weights = []
        for (j, hh), z, sp, after in zip(chains, zs, sps, afters):
            c, acc = state[hh]
            weights.append(jnp.exp(z - sp + (after + c)).astype(BF16))
            state[hh] = (c - (j >= 0).astype(F32) * jnp.sum(sp, axis=0, keepdims=True), acc)
        for (j, hh), a in zip(chains, weights):
            j0 = pl.multiple_of(jnp.maximum(j, 0) * tq, tq)
            pv = jnp.dot(vt_scr[hh * HEAD_DIM:(hh + 1) * HEAD_DIM, pl.ds(j0, tq)], a, preferred_element_type=F32)
            c, acc = state[hh]
            state[hh] = (c, acc + (j >= 0).astype(F32) * pv)
        return tuple(state)

    final = lax.fori_loop(0, (t + 1) // 2, body, tuple(init))
    o_ref[...] = jnp.concatenate([acc_f for _, acc_f in final], axis=0).T


def _sb_prompt(q, k, kvt, batch, seq):
    tq = 256
    n_tiles = seq // tq
    pairs = D_SB // LANES
    return pl.pallas_call(
        functools.partial(_sb_prompt_kernel, n_tiles=n_tiles, tq=tq),
        grid=(batch, pairs, n_tiles),
        in_specs=[pl.BlockSpec((tq, LANES), lambda b, p, t: (b * n_tiles + t, p)),
                  pl.BlockSpec((seq, LANES), lambda b, p, t: (b, p)),
                  pl.BlockSpec((None, LANES, seq), lambda b, p, t: (b, pairs + p, 0))],
        out_specs=pl.BlockSpec((tq, LANES), lambda b, p, t: (b * n_tiles + t, p)),
        out_shape=jax.ShapeDtypeStruct((batch * seq, D_SB), F32),
        scratch_shapes=[pltpu.VMEM((LANES, seq), BF16)],
        compiler_params=_cparams(("parallel", "parallel", "arbitrary")),
        name="sb_prompt",
    )(q, k, kvt)


def _nsa_prompt_kernel(slopes_ref, q_ref, gates_ref, cmp_ref, ksel_ref, kwin_ref, vtsel_ref, vtwin_ref, ovl_ref, o_ref,
                       vcmp_scr, vsel_scr, vwin_scr, bias_scr, *, seq):
    t = pl.program_id(1)
    tq = LANES
    n_tiles = seq // tq
    n_rows_cmp = seq // CMP_STRIDE
    n_cmp = (seq - CMP_LEN) // CMP_STRIDE + 1
    n_sel_blocks = seq // SEL_BLOCK
    n_top = min(SEL_TOPK, n_sel_blocks)
    wide = GROUP * tq

    @pl.when(t == 0)
    def _():
        vcmp_scr[...] = cmp_ref[:, LANES:2 * LANES].T.astype(BF16)
        for n in range(n_tiles):
            vsel_scr[:, n * tq:(n + 1) * tq] = vtsel_ref[:, n * tq:(n + 1) * tq].astype(BF16)
            vwin_scr[:, n * tq:(n + 1) * tq] = vtwin_ref[:, n * tq:(n + 1) * tq].astype(BF16)

    lane = lax.broadcasted_iota(jnp.int32, (1, LANES), 1)
    qi = lax.broadcasted_iota(jnp.int32, (1, wide), 1) % tq
    qpos = t * tq + qi
    krow = lax.broadcasted_iota(jnp.int32, (tq, wide), 0)
    d0 = qi - krow
    q = q_ref[...]
    gates_t = jax.nn.sigmoid(gates_ref[...]).T
    t0 = pl.multiple_of(t * tq, tq)

    qbs, slopes, o_cmps = [], [], []
    for g in range(KV_NSA):
        glanes = (lane < HEAD_DIM) if g == 0 else (lane >= HEAD_DIM)
        qg = jnp.concatenate([jnp.where(glanes, q[:, r * LANES:(r + 1) * LANES], 0.0) for r in range(GROUP)], axis=0)
        qb = (qg * SCALE).astype(BF16)
        slope = jnp.concatenate([jnp.full((1, tq), slopes_ref[g * GROUP + r], F32) for r in range(GROUP)], axis=1)
        vrows = slice(g * HEAD_DIM, (g + 1) * HEAD_DIM)

        nrow = lax.broadcasted_iota(jnp.int32, (n_rows_cmp, wide), 0)
        dist_c = qpos - (nrow * CMP_STRIDE + (CMP_LEN - 1))
        vis_c = (dist_c >= 0) & (nrow < n_cmp)
        sc = _nt_dot(cmp_ref[:, 0:LANES].astype(BF16), qb) - slope * dist_c.astype(F32)
        sc = jnp.where(vis_c, sc, NEG)
        pc = jnp.where(vis_c, jnp.exp(sc - jnp.max(sc, axis=0, keepdims=True)), 0.0)
        lc = jnp.sum(pc, axis=0, keepdims=True)
        pc = pc / jnp.where(lc > 0.0, lc, 1.0)
        o_cmp = jnp.dot(vcmp_scr[vrows, :], pc.astype(BF16), preferred_element_type=F32)

        psum = pc[:, 0:tq]
        for r in range(1, GROUP):
            psum = psum + pc[:, r * tq:(r + 1) * tq]
        imp = jnp.dot(ovl_ref[...], psum, preferred_element_type=F32, precision=lax.Precision.HIGHEST)
        blk = lax.broadcasted_iota(jnp.int32, (n_sel_blocks, tq), 0)
        own = qpos[:, 0:tq] // SEL_BLOCK
        cand = blk <= own
        score = jnp.where((blk == own) | (blk == 0), BIG, jnp.where(cand, imp, -BIG))
        rank = jnp.zeros((n_sel_blocks, tq), jnp.int32)
        for m in range(n_sel_blocks):
            sm = score[m:m + 1, :]
            rank = rank + ((sm > score) | ((sm == score) & (m < blk))).astype(jnp.int32)
        bias_scr[g] = jnp.where(cand & (rank < n_top), 0.0, NEG)
        qbs.append(qb)
        slopes.append(slope)
        o_cmps.append(o_cmp)

    def run_chains(chains, k_ref_, vt_scr_, mask_fn, carry):
        state = list(carry)
        scores = []
        for j, g, _ in chains:
            j0 = pl.multiple_of(j * tq, tq)
            scores.append(_nt_dot(k_ref_[pl.ds(j0, tq), :].astype(BF16), qbs[g]))
        probs = []
        for (j, g, extra), s in zip(chains, scores):
            sj = mask_fn(j, g, s, extra)
            m_j = jnp.max(sj, axis=0, keepdims=True)
            pj = jnp.exp(sj - m_j)
            probs.append((m_j, jnp.sum(pj, axis=0, keepdims=True), pj.astype(BF16)))
        for (j, g, _), (m_j, l_j, pj) in zip(chains, probs):
            j0 = pl.multiple_of(j * tq, tq)
            pv = jnp.dot(vt_scr_[g * HEAD_DIM:(g + 1) * HEAD_DIM, pl.ds(j0, tq)], pj, preferred_element_type=F32)
            state[g] = _merge_softmax(state[g], (m_j, l_j, pv))
        return tuple(state)

    def sel_mask(j, g, s, _):
        b0 = bias_scr[g, pl.ds(2 * j, 1), :]
        b1 = bias_scr[g, pl.ds(2 * j + 1, 1), :]
        bias = jnp.where(krow[:, 0:tq] < SEL_BLOCK, b0, b1)
        bias = jnp.concatenate([bias] * GROUP, axis=1)
        dist = d0 + (t - j) * tq
        return jnp.where(dist >= 0, s - slopes[g] * dist.astype(F32) + bias, NEG)

    def sel_body(i, carry):
        chains = [(2 * i + u, g, None) for u in range(2) for g in range(KV_NSA)]
        return run_chains(chains, ksel_ref, vsel_scr, sel_mask, carry)

    init = (jnp.full((1, wide), NEG, F32), jnp.zeros((1, wide), F32), jnp.zeros((HEAD_DIM, wide), F32))
    sel = lax.fori_loop(0, (t + 2) // 2, sel_body, (init,) * KV_NSA)

    n_early = jnp.minimum(t, WINDOW // tq)

    def win_mask(j, g, s, k):
        dist = d0 + (t - j) * tq
        visible = (dist >= 0) & (dist < WINDOW) & (k <= n_early)
        return jnp.where(visible, s - slopes[g] * dist.astype(F32), NEG)

    def win_tiles(first, count, carry):
        chains = [(jnp.maximum(t - (first + u), 0), g, first + u) for u in range(count) for g in range(KV_NSA)]
        return run_chains(chains, kwin_ref, vwin_scr, win_mask, carry)

    win = lax.fori_loop(0, (n_early + 1) // 2, lambda i, c: win_tiles(1 + 2 * i, 2, c),
                        win_tiles(0, 1, (init,) * KV_NSA))

    for r in range(GROUP):
        halves = []
        for g in range(KV_NSA):
            h = g * GROUP + r
            sl = slice(r * tq, (r + 1) * tq)
            o_sel = sel[g][2][:, sl] / sel[g][1][:, sl]
            o_win = win[g][2][:, sl] / win[g][1][:, sl]
            halves.append(gates_t[h:h + 1, :] * o_cmps[g][:, sl] + gates_t[H_NSA + h:H_NSA + h + 1, :] * o_sel
                          + gates_t[2 * H_NSA + h:2 * H_NSA + h + 1, :] * o_win)
        o_ref[:, r * LANES:(r + 1) * LANES] = jnp.concatenate(halves, axis=0).T


def _nsa_prompt(q, gates, kcv, k_sel, k_win, kvt_sel, kvt_win, overlap_t, slopes, batch, seq):
    tq = LANES
    n_tiles = seq // tq
    n_rows_cmp = seq // CMP_STRIDE
    return pl.pallas_call(
        functools.partial(_nsa_prompt_kernel, seq=seq),
        grid=(batch, n_tiles),
        in_specs=[pl.BlockSpec(memory_space=pltpu.SMEM),
                  pl.BlockSpec((tq, D_NSA), lambda b, t: (b * n_tiles + t, 0)),
                  pl.BlockSpec((tq, LANES), lambda b, t: (b * n_tiles + t, 0)),
                  pl.BlockSpec((n_rows_cmp, 2 * LANES), lambda b, t: (b, 0)),
                  pl.BlockSpec((seq, LANES), lambda b, t: (b, 0)),
                  pl.BlockSpec((seq, LANES), lambda b, t: (b, 0)),
                  pl.BlockSpec((None, LANES, seq), lambda b, t: (b, 1, 0)),
                  pl.BlockSpec((None, LANES, seq), lambda b, t: (b, 1, 0)),
                  pl.BlockSpec(overlap_t.shape, lambda b, t: (0, 0))],
        out_specs=pl.BlockSpec((tq, D_NSA), lambda b, t: (b * n_tiles + t, 0)),
        out_shape=jax.ShapeDtypeStruct((batch * seq, D_NSA), F32),
        scratch_shapes=[pltpu.VMEM((LANES, n_rows_cmp), BF16),
                        pltpu.VMEM((LANES, seq), BF16),
                        pltpu.VMEM((LANES, seq), BF16),
                        pltpu.VMEM((KV_NSA, seq // SEL_BLOCK, tq), F32)],
        compiler_params=_cparams(("parallel", "arbitrary")),
        name="nsa_prompt",
    )(slopes, q, gates, kcv, k_sel, k_win, kvt_sel, kvt_win, overlap_t)


def _head_rows(q8, n_heads, lane_of_head):
    lanes = lax.broadcasted_iota(jnp.int32, (1, q8.shape[1]), 1)
    rows = []
    for h in range(n_heads):
        lo = lane_of_head(h)
        rows.append(jnp.where((lanes >= lo) & (lanes < lo + HEAD_DIM), q8, 0.0))
    return jnp.concatenate(rows, axis=0)


def _row_const(values, n_rows):
    row = lax.broadcasted_iota(jnp.int32, (n_rows, 1), 0) // Q_ROWS
    out = jnp.zeros((n_rows, 1), F32)
    for h, v in enumerate(values):
        out = jnp.where(row == h, v, out)
    return out


def _fold_heads(o, n_heads, lane_of_head):
    lanes = lax.broadcasted_iota(jnp.int32, (1, o.shape[1]), 1)
    out = jnp.zeros((Q_ROWS, o.shape[1]), F32)
    for h in range(n_heads):
        lo = lane_of_head(h)
        out = out + jnp.where((lanes >= lo) & (lanes < lo + HEAD_DIM), o[h * Q_ROWS:(h + 1) * Q_ROWS, :], 0.0)
    return out


def _moba_decode_kernel(pt_ref, slopes_ref, q_ref, new_ref, *refs, n_pages, npp, past):
    del pt_ref
    page_refs = refs[:npp]
    o_ref, qb_scr, slope_scr, new_scr, m_scr, l_scr, acc_scr, ksum_scr = refs[npp:]
    s = pl.program_id(1)
    rows = H_MOBA * Q_ROWS
    pages_per_block = MOBA_BLOCK // PAGE
    n_blocks = n_pages // pages_per_block
    lane_of_head = lambda h: h * HEAD_DIM
    qoff = lax.broadcasted_iota(jnp.int32, (rows, 1), 0) % Q_ROWS
    key = lax.broadcasted_iota(jnp.int32, (rows, PAGE), 1)
    klane = lax.broadcasted_iota(jnp.int32, (D_MOBA, LANES), 1)

    @pl.when(s == 0)
    def _():
        qb_scr[...] = (_head_rows(q_ref[...], H_MOBA, lane_of_head) * SCALE).astype(BF16)
        slope_scr[...] = jnp.broadcast_to(_row_const([slopes_ref[h] for h in range(H_MOBA)], rows), (rows, PAGE))
        ksum_scr[...] = jnp.zeros(ksum_scr.shape, F32)
        new_scr[...] = jnp.zeros(new_scr.shape, F32)
        new_scr[0:NEW_ROWS, :] = new_ref[...]
        dist = qoff - key
        sc = _nt_dot(qb_scr[...], new_scr[:, 0:D_MOBA].astype(BF16)) - slope_scr[...] * dist.astype(F32)
        sc = jnp.where(dist >= 0, sc, NEG)
        m = jnp.max(sc, axis=1, keepdims=True)
        pe = jnp.exp(sc - m)
        acc_scr[n_pages] = jnp.dot(pe.astype(BF16), new_scr[:, D_MOBA:2 * D_MOBA].astype(BF16),
                                   preferred_element_type=F32)
        m_scr[...] = jnp.where(key == n_pages, m, NEG)
        l_scr[...] = jnp.where(key == n_pages, jnp.sum(pe, axis=1, keepdims=True), 0.0)

    m_all = m_scr[...]
    l_all = l_scr[...]
    ksum = jnp.zeros((D_MOBA, LANES), F32)
    qb = qb_scr[...]
    scores = [jnp.dot(qb, page_refs[i][0:D_MOBA, :].astype(BF16), preferred_element_type=F32)
              for i in range(npp)]
    probs = []
    for i in range(npp):
        slot = s * npp + i
        dist = (past + qoff) - (slot * PAGE + key)
        sc = scores[i] - slope_scr[...] * dist.astype(F32)
        m = jnp.max(sc, axis=1, keepdims=True)
        pe = jnp.exp(sc - m)
        probs.append(pe.astype(BF16))
        m_all = jnp.where(key == slot, m, m_all)
        l_all = jnp.where(key == slot, jnp.sum(pe, axis=1, keepdims=True), l_all)
    for i in range(npp):
        acc_scr[s * npp + i] = _nt_dot(probs[i], page_refs[i][D_MOBA:2 * D_MOBA, :].astype(BF16))
    for i in range(npp):
        slot = s * npp + i
        kt = page_refs[i][0:D_MOBA, :]
        ksum = ksum + jnp.where(klane == slot // pages_per_block, jnp.sum(kt, axis=1, keepdims=True), 0.0)
    m_scr[...] = m_all
    l_scr[...] = l_all
    ksum_scr[...] = ksum_scr[...] + ksum

    @pl.when(s == pl.num_programs(1) - 1)
    def _():
        qf = _head_rows(q_ref[...], H_MOBA, lane_of_head)
        gate = jnp.dot(qf, ksum_scr[...] * (1.0 / MOBA_BLOCK), preferred_element_type=F32,
                       precision=lax.Precision.HIGHEST)
        is_block = key < n_blocks
        gate = jnp.where(is_block, gate, NEG)
        rank = jnp.zeros((rows, LANES), jnp.int32)
        for m in range(n_blocks):
            gm = gate[:, m:m + 1]
            rank = rank + ((gm > gate) | ((gm == gate) & (m < key))).astype(jnp.int32)
        chosen = (is_block & (rank < MOBA_TOPK)).astype(F32)
        blk_i = lax.broadcasted_iota(jnp.int32, (LANES, LANES), 0)
        slot_i = lax.broadcasted_iota(jnp.int32, (LANES, LANES), 1)
        expand = ((slot_i // pages_per_block == blk_i) & (slot_i < n_pages)).astype(F32)
        use = (jnp.dot(chosen, expand, preferred_element_type=F32) > 0.5) | (key == n_pages)
        m_tot = jnp.max(jnp.where(use, m_all, NEG), axis=1, keepdims=True)
        w = jnp.where(use, jnp.exp(m_all - m_tot), 0.0)
        den = jnp.sum(w * l_all, axis=1, keepdims=True)
        num = jnp.zeros((rows, D_MOBA), F32)
        for slot in range(n_pages + 1):
            num = num + w[:, slot:slot + 1] * acc_scr[slot]
        o_ref[...] = _fold_heads(num / den, H_MOBA, lane_of_head)


def _page_specs(layer, rows, n_pages, npp, reverse=False):
    def spec(i):
        if reverse:
            return pl.BlockSpec((None, None, rows, PAGE),
                                lambda b, s, pt: (layer, pt[b * n_pages + (n_pages - 1 - (s * npp + i))], 0, 0))
        return pl.BlockSpec((None, None, rows, PAGE), lambda b, s, pt: (layer, pt[b * n_pages + s * npp + i], 0, 0))
    return [spec(i) for i in range(npp)]


def _moba_decode(q8, new16, pool, layer, page_ids, slopes, n_req, n_pages):
    rows = H_MOBA * Q_ROWS
    past = n_pages * PAGE
    npp = min(PAGES_PER_STEP, n_pages)
    assert n_pages + 1 <= LANES
    grid_spec = pltpu.PrefetchScalarGridSpec(
        num_scalar_prefetch=1,
        grid=(n_req, n_pages // npp),
        in_specs=[pl.BlockSpec(memory_space=pltpu.SMEM),
                  pl.BlockSpec((None, Q_ROWS, D_MOBA), lambda b, s, pt: (b, 0, 0)),
                  pl.BlockSpec((None, NEW_ROWS, 2 * D_MOBA), lambda b, s, pt: (b, 0, 0))]
                 + _page_specs(layer, 2 * D_MOBA, n_pages, npp),
        out_specs=pl.BlockSpec((None, Q_ROWS, D_MOBA), lambda b, s, pt: (b, 0, 0)),
        scratch_shapes=[pltpu.VMEM((rows, D_MOBA), BF16),
                        pltpu.VMEM((rows, PAGE), F32),
                        pltpu.VMEM((PAGE, 2 * D_MOBA), F32),
                        pltpu.VMEM((rows, LANES), F32),
                        pltpu.VMEM((rows, LANES), F32),
                        pltpu.VMEM((n_pages + 1, rows, D_MOBA), F32),
                        pltpu.VMEM((D_MOBA, LANES), F32)],
    )
    return pl.pallas_call(
        functools.partial(_moba_decode_kernel, n_pages=n_pages, npp=npp, past=past),
        grid_spec=grid_spec,
        out_shape=jax.ShapeDtypeStruct((n_req, Q_ROWS, D_MOBA), F32),
        compiler_params=_cparams(("parallel", "arbitrary")),
        name="moba_decode",
    )(page_ids, slopes, q8, new16, *([pool] * npp))


def _sb_decode_kernel(pt_ref, q_ref, new_ref, *refs, npp):
    del pt_ref
    page_refs = refs[:npp]
    o_ref, qb_scr, new_scr, c_scr, acc_scr = refs[npp:]
    p = pl.program_id(1)
    rows = H_SB * Q_ROWS
    lane_of_head = lambda h: h * HEAD_DIM
    qoff = lax.broadcasted_iota(jnp.int32, (rows, 1), 0) % Q_ROWS
    key = lax.broadcasted_iota(jnp.int32, (rows, PAGE), 1)
    kj = lax.broadcasted_iota(jnp.int32, (PAGE, PAGE), 0)
    ks = lax.broadcasted_iota(jnp.int32, (PAGE, PAGE), 1)
    after_mat = (kj > ks).astype(BF16)

    def page_update(z, c, acc, causal, pv):
        sp = _softplus(z)
        lgt = -sp if causal is None else jnp.where(causal, -sp, 0.0)
        hi = lgt.astype(BF16)
        lo = (lgt - hi.astype(F32)).astype(BF16)
        after = (jnp.dot(hi, after_mat, preferred_element_type=F32)
                 + jnp.dot(lo, after_mat, preferred_element_type=F32)) + c
        a = jnp.exp(z - sp + after)
        if causal is not None:
            a = jnp.where(causal, a, 0.0)
        return c + jnp.sum(lgt, axis=1, keepdims=True), acc + pv(a.astype(BF16))

    @pl.when(p == 0)
    def _():
        qb_scr[...] = (_head_rows(q_ref[...], H_SB, lane_of_head) * SCALE).astype(BF16)
        new_scr[...] = jnp.zeros(new_scr.shape, F32)
        new_scr[0:NEW_ROWS, :] = new_ref[...]
        c0, acc0 = page_update(
            _nt_dot(qb_scr[...], new_scr[:, 0:D_SB].astype(BF16)),
            jnp.zeros((rows, PAGE), F32), jnp.zeros((rows, D_SB), F32), key < qoff,
            lambda a: jnp.dot(a, new_scr[:, D_SB:2 * D_SB].astype(BF16), preferred_element_type=F32))
        c_scr[...] = c0
        acc_scr[...] = acc0

    c = c_scr[...]
    acc = acc_scr[...]
    qb = qb_scr[...]
    zs = [jnp.dot(qb, page_refs[i][0:D_SB, :].astype(BF16), preferred_element_type=F32)
          for i in range(npp)]
    sps = [_softplus(z) for z in zs]
    afters = []
    for sp in sps:
        lgt = -sp
        hi = lgt.astype(BF16)
        lo = (lgt - hi.astype(F32)).astype(BF16)
        afters.append(jnp.dot(hi, after_mat, preferred_element_type=F32)
                      + jnp.dot(lo, after_mat, preferred_element_type=F32))
    weights = []
    for z, sp, after in zip(zs, sps, afters):
        weights.append(jnp.exp(z - sp + (after + c)).astype(BF16))
        c = c - jnp.sum(sp, axis=1, keepdims=True)
    for i in range(npp):
        acc = acc + _nt_dot(weights[i], page_refs[i][D_SB:2 * D_SB, :].astype(BF16))
    c_scr[...] = c
    acc_scr[...] = acc

    @pl.when(p == pl.num_programs(1) - 1)
    def _():
        o_ref[...] = _fold_heads(acc, H_SB, lane_of_head)


def _sb_decode(q8, new16, pool, layer, page_ids, n_req, n_pages):
    rows = H_SB * Q_ROWS
    npp = min(PAGES_PER_STEP, n_pages)
    grid_spec = pltpu.PrefetchScalarGridSpec(
        num_scalar_prefetch=1,
        grid=(n_req, n_pages // npp),
        in_specs=[pl.BlockSpec((None, Q_ROWS, D_SB), lambda b, p, pt: (b, 0, 0)),
                  pl.BlockSpec((None, NEW_ROWS, 2 * D_SB), lambda b, p, pt: (b, 0, 0))]
                 + _page_specs(layer, 2 * D_SB, n_pages, npp, reverse=True),
        out_specs=pl.BlockSpec((None, Q_ROWS, D_SB), lambda b, p, pt: (b, 0, 0)),
        scratch_shapes=[pltpu.VMEM((rows, D_SB), BF16),
                        pltpu.VMEM((PAGE, 2 * D_SB), F32),
                        pltpu.VMEM((rows, PAGE), F32),
                        pltpu.VMEM((rows, D_SB), F32)],
    )
    return pl.pallas_call(
        functools.partial(_sb_decode_kernel, npp=npp),
        grid_spec=grid_spec,
        out_shape=jax.ShapeDtypeStruct((n_req, Q_ROWS, D_SB), F32),
        compiler_params=_cparams(("parallel", "arbitrary")),
        name="sb_decode",
    )(page_ids, q8, new16, *([pool] * npp))


def _nsa_decode_kernel(pt_ref, slopes_ref, q_ref, gates_ref, cmp_ref, selnew_ref, winbuf_ref, winnew_ref, ovl_ref,
                       *refs, npp, past):
    del pt_ref
    page_refs = refs[:npp]
    o_ref, qb_scr, slope_scr, new_scr, drop_scr, ocmp_scr, owin_scr, m_scr, l_scr, acc_scr = refs[npp:]
    p = pl.program_id(1)
    rows = H_NSA * Q_ROWS
    grows = KV_NSA * Q_ROWS
    n_rows_cmp = cmp_ref.shape[0]
    total = past + Q_ROWS // 2
    n_cmp = (total - CMP_LEN) // CMP_STRIDE + 1
    n_sel_blocks = -(-total // SEL_BLOCK)
    n_top = min(SEL_TOPK, n_sel_blocks)
    win_buf = winbuf_ref.shape[1]
    lane_of_head = lambda h: (h // GROUP) * HEAD_DIM
    qoff = lax.broadcasted_iota(jnp.int32, (rows, 1), 0) % Q_ROWS
    key = lax.broadcasted_iota(jnp.int32, (rows, PAGE), 1)

    @pl.when(p == 0)
    def _():
        q = q_ref[...]
        qh = jnp.concatenate([q[:, (h % GROUP) * LANES:(h % GROUP + 1) * LANES] for h in range(H_NSA)], axis=0)
        lanes = lax.broadcasted_iota(jnp.int32, (rows, LANES), 1)
        grp = lax.broadcasted_iota(jnp.int32, (rows, LANES), 0) // (GROUP * Q_ROWS)
        qh = jnp.where((lanes // HEAD_DIM) == grp, qh, 0.0)
        qb = (qh * SCALE).astype(BF16)
        qb_scr[...] = qb
        slope = _row_const([slopes_ref[h] for h in range(H_NSA)], rows)
        slope_scr[...] = jnp.broadcast_to(slope, (rows, PAGE))

        ncol = lax.broadcasted_iota(jnp.int32, (rows, n_rows_cmp), 1)
        dist_c = (past + qoff) - (ncol * CMP_STRIDE + (CMP_LEN - 1))
        vis_c = (dist_c >= 0) & (ncol < n_cmp)
        sc = _nt_dot(qb, cmp_ref[:, 0:LANES].astype(BF16)) - slope * dist_c.astype(F32)
        sc = jnp.where(vis_c, sc, NEG)
        pc = jnp.where(vis_c, jnp.exp(sc - jnp.max(sc, axis=1, keepdims=True)), 0.0)
        lc = jnp.sum(pc, axis=1, keepdims=True)
        pc = pc / jnp.where(lc > 0.0, lc, 1.0)
        ocmp_scr[...] = jnp.dot(pc.astype(BF16), cmp_ref[:, LANES:2 * LANES].astype(BF16), preferred_element_type=F32)

        psum = []
        for g in range(KV_NSA):
            acc = pc[g * GROUP * Q_ROWS:(g * GROUP + 1) * Q_ROWS, :]
            for r in range(1, GROUP):
                acc = acc + pc[(g * GROUP + r) * Q_ROWS:(g * GROUP + r + 1) * Q_ROWS, :]
            psum.append(acc)
        imp = jnp.dot(jnp.concatenate(psum, axis=0), ovl_ref[...], preferred_element_type=F32,
                      precision=lax.Precision.HIGHEST)
        blk = lax.broadcasted_iota(jnp.int32, (grows, LANES), 1)
        own = (past + lax.broadcasted_iota(jnp.int32, (grows, 1), 0) % Q_ROWS) // SEL_BLOCK
        cand = blk <= own
        score = jnp.where((blk == own) | (blk == 0), BIG, jnp.where(cand, imp, -BIG))
        rank = jnp.zeros((grows, LANES), jnp.int32)
        for m in range(n_sel_blocks):
            sm = score[:, m:m + 1]
            rank = rank + ((sm > score) | ((sm == score) & (m < blk))).astype(jnp.int32)
        drop_g = jnp.where(cand & (rank < n_top), 0.0, 1.0)
        drop_scr[...] = jnp.concatenate([drop_g[(h // GROUP) * Q_ROWS:(h // GROUP + 1) * Q_ROWS, :]
                                         for h in range(H_NSA)], axis=0).astype(BF16)

        new_scr[...] = jnp.zeros(new_scr.shape, F32)
        new_scr[0:NEW_ROWS, :] = winnew_ref[...]
        wkey = lax.broadcasted_iota(jnp.int32, (rows, win_buf), 1)
        dist_b = qoff + (win_buf - wkey)
        sb = (jnp.dot(qb, winbuf_ref[0:LANES, :].astype(BF16), preferred_element_type=F32)
              - slope * dist_b.astype(F32))
        sb = jnp.where((dist_b < WINDOW) & (past - win_buf + wkey >= 0), sb, NEG)
        dist_n = qoff - key
        sn = _nt_dot(qb, new_scr[:, 0:LANES].astype(BF16)) - slope * dist_n.astype(F32)
        sn = jnp.where(dist_n >= 0, sn, NEG)
        mw = jnp.maximum(jnp.max(sb, axis=1, keepdims=True), jnp.max(sn, axis=1, keepdims=True))
        pb = jnp.exp(sb - mw)
        pn = jnp.exp(sn - mw)
        lw = jnp.sum(pb, axis=1, keepdims=True) + jnp.sum(pn, axis=1, keepdims=True)
        ow = (_nt_dot(pb.astype(BF16), winbuf_ref[LANES:2 * LANES, :].astype(BF16))
              + jnp.dot(pn.astype(BF16), new_scr[:, LANES:2 * LANES].astype(BF16), preferred_element_type=F32))
        owin_scr[...] = ow / lw

        new_scr[0:NEW_ROWS, :] = selnew_ref[...]
        ss = _nt_dot(qb, new_scr[:, 0:LANES].astype(BF16)) - slope * dist_n.astype(F32)
        ss = jnp.where(dist_n >= 0, ss, NEG)
        m0 = jnp.max(ss, axis=1, keepdims=True)
        pe0 = jnp.exp(ss - m0)
        m_scr[...] = jnp.broadcast_to(m0, (rows, LANES))
        l_scr[...] = jnp.broadcast_to(jnp.sum(pe0, axis=1, keepdims=True), (rows, LANES))
        acc_scr[...] = jnp.dot(pe0.astype(BF16), new_scr[:, LANES:2 * LANES].astype(BF16), preferred_element_type=F32)

    step_tokens = npp * PAGE
    blk_i = lax.broadcasted_iota(jnp.int32, (LANES, step_tokens), 0)
    tok_i = lax.broadcasted_iota(jnp.int32, (LANES, step_tokens), 1)
    spread = (blk_i == p * (step_tokens // SEL_BLOCK) + tok_i // SEL_BLOCK).astype(BF16)
    drop = jnp.dot(drop_scr[...], spread, preferred_element_type=F32)
    scores = []
    for i in range(npp):
        kt = page_refs[i][0:LANES, :]
        dist = (past + qoff) - ((p * npp + i) * PAGE + key)
        sc = jnp.dot(qb_scr[...], kt.astype(BF16), preferred_element_type=F32) - slope_scr[...] * dist.astype(F32)
        scores.append(jnp.where(drop[:, i * PAGE:(i + 1) * PAGE] > 0.5, NEG, sc))
    m_old = m_scr[...][:, 0:1]
    m_new = m_old
    for sc in scores:
        m_new = jnp.maximum(m_new, jnp.max(sc, axis=1, keepdims=True))
    alpha = jnp.exp(m_old - m_new)
    l_new = alpha * l_scr[...][:, 0:1]
    acc = alpha * acc_scr[...]
    probs = []
    for sc in scores:
        pe = jnp.exp(sc - m_new)
        l_new = l_new + jnp.sum(pe, axis=1, keepdims=True)
        probs.append(pe.astype(BF16))
    for i in range(npp):
        acc = acc + _nt_dot(probs[i], page_refs[i][LANES:2 * LANES, :].astype(BF16))
    m_scr[...] = jnp.broadcast_to(m_new, (rows, LANES))
    l_scr[...] = jnp.broadcast_to(l_new, (rows, LANES))
    acc_scr[...] = acc

    @pl.when(p == pl.num_programs(1) - 1)
    def _():
        o_sel = acc_scr[...] / l_scr[...][:, 0:1]
        gts = jax.nn.sigmoid(gates_ref[...])
        lane = lax.broadcasted_iota(jnp.int32, (1, LANES), 1)
        heads = []
        for h in range(H_NSA):
            sl = slice(h * Q_ROWS, (h + 1) * Q_ROWS)
            heads.append(gts[:, h:h + 1] * ocmp_scr[sl, :] + gts[:, H_NSA + h:H_NSA + h + 1] * o_sel[sl, :]
                         + gts[:, 2 * H_NSA + h:2 * H_NSA + h + 1] * owin_scr[sl, :])
        for r in range(GROUP):
            o_ref[:, r * LANES:(r + 1) * LANES] = jnp.where(lane < HEAD_DIM, heads[r], heads[GROUP + r])


def _nsa_decode(q8, gates8, kcv, selnew16, winbuf, winnew16, overlap, pool, layer, page_ids, slopes, n_req, n_pages):
    rows = H_NSA * Q_ROWS
    n_rows_cmp = n_pages * (PAGE // CMP_STRIDE)
    win_buf = winbuf.shape[3]
    npp = min(PAGES_PER_STEP, n_pages)
    assert -(-(n_pages * PAGE + Q_ROWS // 2) // SEL_BLOCK) <= LANES
    per_req = lambda shape: pl.BlockSpec((None,) + shape, lambda b, p, pt: (b, 0, 0))
    grid_spec = pltpu.PrefetchScalarGridSpec(
        num_scalar_prefetch=1,
        grid=(n_req, n_pages // npp),
        in_specs=[pl.BlockSpec(memory_space=pltpu.SMEM),
                  per_req((Q_ROWS, D_NSA)),
                  per_req((Q_ROWS, LANES)),
                  pl.BlockSpec((n_rows_cmp, 2 * LANES), lambda b, p, pt: (b, 0)),
                  per_req((NEW_ROWS, 2 * LANES)),
                  pl.BlockSpec((None, None, 2 * LANES, win_buf), lambda b, p, pt: (layer, b, 0, 0)),
                  per_req((NEW_ROWS, 2 * LANES)),
                  pl.BlockSpec(overlap.shape, lambda b, p, pt: (0, 0))]
                 + _page_specs(layer, 2 * LANES, n_pages, npp),
        out_specs=per_req((Q_ROWS, D_NSA)),
        scratch_shapes=[pltpu.VMEM((rows, LANES), BF16),
                        pltpu.VMEM((rows, PAGE), F32),
                        pltpu.VMEM((PAGE, 2 * LANES), F32),
                        pltpu.VMEM((rows, LANES), BF16),
                        pltpu.VMEM((rows, LANES), F32),
                        pltpu.VMEM((rows, LANES), F32),
                        pltpu.VMEM((rows, LANES), F32),
                        pltpu.VMEM((rows, LANES), F32),
                        pltpu.VMEM((rows, LANES), F32)],
    )
    return pl.pallas_call(
        functools.partial(_nsa_decode_kernel, npp=npp, past=n_pages * PAGE),
        grid_spec=grid_spec,
        out_shape=jax.ShapeDtypeStruct((n_req, Q_ROWS, D_NSA), F32),
        compiler_params=_cparams(("parallel", "arbitrary")),
        name="nsa_decode",
    )(page_ids, slopes, q8, gates8, kcv, selnew16, winbuf, winnew16, overlap, *([pool] * npp))


def _prep_layer_weights(w_in, cmp_pos, w_ck1, w_ck2, w_cv1, w_cv2, norm_mix, w_out):
    sizes = (D_MOBA, D_MOBA, D_MOBA, D_NSA, D_NSA_KV, D_NSA_KV, D_NSA_KV, D_NSA_KV, D_NSA_KV, D_NSA_KV,
             N_GATES, D_SB, D_SB, D_SB)
    offs = np.concatenate([[0], np.cumsum(sizes)])
    (qa, ka, va, qb, kc, vc, ks, vs, kw, vw, gt, qc, kcs, vcs) = [np.arange(offs[i], offs[i + 1]) for i in range(14)]
    qb_perm = np.concatenate([qb[h * HEAD_DIM:(h + 1) * HEAD_DIM] for h in _NSA_HEAD_ORDER])
    gate_pad = ((0, 0), (0, 0), (0, LANES - N_GATES))
    take = lambda cols: jnp.take(w_in, jnp.asarray(np.concatenate(cols)), axis=-1)
    w_tok_s = jnp.pad(take([qa, ka, va, qb_perm, kc, vc, ks, vs, kw, vw, qc, kcs, vcs, gt]), gate_pad).astype(BF16)
    w_tok_p = jnp.pad(take([qa, ka, qb_perm, ks, kw, qc, kcs, gt]), gate_pad).astype(BF16)
    w_feat_p = jnp.swapaxes(take([ka, va, kc, vc, ks, vs, kw, vw, kcs, vcs]), 1, 2).astype(BF16)
    w = (w_tok_s, w_tok_p, w_feat_p)

    nsa_perm = np.concatenate([D_MOBA + np.arange(h * HEAD_DIM, (h + 1) * HEAD_DIM) for h in _NSA_HEAD_ORDER])
    mix_perm = jnp.asarray(np.concatenate([np.arange(D_MOBA), nsa_perm, np.arange(D_MOBA + D_NSA, D_MOBA + D_NSA + D_SB)]))
    g_mix = jnp.take(norm_mix, mix_perm, axis=-1)
    wo = jnp.take(w_out, mix_perm, axis=-2).astype(BF16)

    depth = w_in.shape[0]
    k1 = w_ck1.reshape(depth, CMP_LEN, HEAD_DIM, CMP_HID)
    v1 = w_cv1.reshape(depth, CMP_LEN, HEAD_DIM, CMP_HID)
    z1 = jnp.zeros_like(k1)
    w1 = jnp.concatenate([jnp.concatenate([k1, z1, z1, z1], axis=-1), jnp.concatenate([z1, k1, z1, z1], axis=-1),
                          jnp.concatenate([z1, z1, v1, z1], axis=-1), jnp.concatenate([z1, z1, z1, v1], axis=-1)],
                         axis=-2).astype(BF16)
    z2 = jnp.zeros_like(w_ck2)
    w2 = jnp.concatenate([jnp.concatenate([w_ck2, z2, z2, z2], axis=-1), jnp.concatenate([z2, w_ck2, z2, z2], axis=-1),
                          jnp.concatenate([z2, z2, w_cv2, z2], axis=-1), jnp.concatenate([z2, z2, z2, w_cv2], axis=-1)],
                         axis=-2).astype(BF16)
    pos = jnp.concatenate([cmp_pos[:, 0], cmp_pos[:, 0], cmp_pos[:, 1], cmp_pos[:, 1]], axis=-1)[:, :, None, :]
    return w, g_mix, wo, w1, w2, pos


def _overlap(n_rows_cmp, n_cmp, n_blocks, n_cols):
    starts = np.arange(n_rows_cmp) * CMP_STRIDE
    sbs = np.arange(n_cols) * SEL_BLOCK
    ov = (starts[:, None] < sbs[None, :] + SEL_BLOCK) & (starts[:, None] + CMP_LEN > sbs[None, :])
    ov &= (np.arange(n_rows_cmp)[:, None] < n_cmp) & (np.arange(n_cols)[None, :] < n_blocks)
    return ov.astype(np.float32)


def kernel(x_prompt, x_sample, cache_moba_kv, cache_nsa_cmp_kv, cache_nsa_sel_kv, cache_sb_kv, state_nsa_win_kv, page_table, norm_attn, w_in, cmp_pos, w_cmp_k1, w_cmp_k2, w_cmp_v1, w_cmp_v2, norm_mix, w_out, norm_ffn, w_up, w_down, norm_final):
    batch, seq, d_model = x_prompt.shape
    n_req, n_new, _ = x_sample.shape
    depth = w_in.shape[0]
    n_pages = page_table.shape[1]
    past = n_pages * PAGE
    n_phys = cache_moba_kv.shape[1]
    assert n_new * 2 == Q_ROWS and seq % MOBA_BLOCK == 0 and past % MOBA_BLOCK == 0

    slopes_a, slopes_b = _alibi_slopes()
    w_proj, g_mix, w_o, w_c1, w_c2, pos_c = _prep_layer_weights(w_in, cmp_pos, w_cmp_k1, w_cmp_k2, w_cmp_v1, w_cmp_v2,
                                                                norm_mix, w_out)
    w_up_b = w_up.astype(BF16)
    w_down_b = w_down.astype(BF16)
    w_tok_s, w_tok_p, w_feat_p = w_proj
    page_ids = page_table.reshape(-1).astype(jnp.int32)

    def token_minor(c):
        lead = c.shape[:2]
        return jnp.transpose(c, (0, 1, 3, 4, 5, 2)).reshape(lead + (-1, c.shape[2]))

    def token_major(s, heads):
        lead, tokens = s.shape[:2], s.shape[3]
        return jnp.transpose(s.reshape(lead + (2, heads, HEAD_DIM, tokens)), (0, 1, 5, 2, 3, 4))

    pool_moba, pool_cmp, pool_sel, pool_sb = (token_minor(c) for c in
                                              (cache_moba_kv, cache_nsa_cmp_kv, cache_nsa_sel_kv, cache_sb_kv))
    win_state = token_minor(state_nsa_win_kv)

    n_rows_p = seq // CMP_STRIDE
    ovl_p = jnp.asarray(_overlap(n_rows_p, (seq - CMP_LEN) // CMP_STRIDE + 1, seq // SEL_BLOCK, seq // SEL_BLOCK).T)
    total = past + n_new
    ovl_s = jnp.asarray(_overlap(past // CMP_STRIDE, (total - CMP_LEN) // CMP_STRIDE + 1, -(-total // SEL_BLOCK), LANES))

    xp = x_prompt.reshape(batch * seq, d_model)
    xs = x_sample.reshape(n_req * n_new, d_model)
    pad_q = lambda a: jnp.pad(a.reshape(n_req, n_new, -1), ((0, 0), (0, Q_ROWS - n_new), (0, 0)))
    pad_new = lambda a: jnp.pad(a.reshape(n_req, n_new, -1), ((0, 0), (0, NEW_ROWS - n_new), (0, 0)))
    st_p = [[] for _ in range(5)]
    st_s = [[] for _ in range(5)]
    win_keep = min(WINDOW, seq)
    for l in range(depth):
        (q_moba, k_moba, q_nsa, k_sel, k_win, q_sb, k_sb, gates, kvt_moba, kvt_cmp, kvt_sel, kvt_win, kvt_sb) = _in_proj(
            xp, norm_attn[l], w_tok_p[l], _PROJ_TOK_P, w_feat_p[l], _PROJ_FEAT_P, seq)
        kcv = _compress(kvt_cmp, l, None, batch, seq // PAGE, pos_c[l], w_c1[l], w_c2[l])
        o_a = _moba_prompt(q_moba, k_moba, kvt_moba, slopes_a, batch, seq)
        o_b = _nsa_prompt(q_nsa, gates, kcv, k_sel, k_win, kvt_sel, kvt_win, ovl_p, slopes_b, batch, seq)
        o_c = _sb_prompt(q_sb, k_sb, kvt_sb, batch, seq)
        xp = _post(o_a, o_b, o_c, xp, g_mix[l], w_o[l], norm_ffn[l], w_up_b[l], w_down_b[l])
        for i, a in enumerate((kvt_moba, kvt_cmp, kvt_sel, kvt_sb, kvt_win[:, :, seq - win_keep:])):
            st_p[i].append(a)

        (q_moba, kv_moba, q_nsa, kv_cmp, kv_sel, kv_win, q_sb, kv_sb, gates) = _in_proj(
            xs, norm_attn[l], w_tok_s[l], _PROJ_TOK_S)
        kcv = _compress(pool_cmp, l, page_ids, n_req, n_pages, pos_c[l], w_c1[l], w_c2[l])
        o_a = _moba_decode(pad_q(q_moba), pad_new(kv_moba), pool_moba, l, page_ids, slopes_a, n_req, n_pages)
        o_b = _nsa_decode(pad_q(q_nsa), pad_q(gates), kcv, pad_new(kv_sel), win_state, pad_new(kv_win), ovl_s,
                          pool_sel, l, page_ids, slopes_b, n_req, n_pages)
        o_c = _sb_decode(pad_q(q_sb), pad_new(kv_sb), pool_sb, l, page_ids, n_req, n_pages)
        unpad = lambda o: o[:, :n_new].reshape(n_req * n_new, -1)
        xs = _post(unpad(o_a), unpad(o_b), unpad(o_c), xs, g_mix[l], w_o[l], norm_ffn[l], w_up_b[l], w_down_b[l])
        new_win_t = jnp.swapaxes(kv_win.reshape(n_req, n_new, 2 * D_NSA_KV), 1, 2)
        all_win = jnp.concatenate([win_state[l], new_win_t], axis=2)
        st_s[0].append(kv_moba.reshape(n_req, n_new, 2, H_MOBA, HEAD_DIM))
        st_s[1].append(kv_cmp.reshape(n_req, n_new, 2, KV_NSA, HEAD_DIM))
        st_s[2].append(kv_sel.reshape(n_req, n_new, 2, KV_NSA, HEAD_DIM))
        st_s[3].append(kv_sb.reshape(n_req, n_new, 2, H_SB, HEAD_DIM))
        st_s[4].append(all_win[:, :, all_win.shape[2] - min(WINDOW, all_win.shape[2]):])

    y_prompt = _final_norm(xp, norm_final).reshape(batch, seq, d_model)
    y_sample = _final_norm(xs, norm_final).reshape(n_req, n_new, d_model)
    heads = (H_MOBA, KV_NSA, KV_NSA, H_SB, KV_NSA)
    new_state = []
    for i in range(5):
        new_state.append(token_major(jnp.stack(st_p[i]), heads[i]))
        new_state.append(token_major(jnp.stack(st_s[i]), heads[i]) if i == 4 else jnp.stack(st_s[i]))
    return (y_prompt, y_sample, *new_state)
```
